```python
import math
import jax
import jax.numpy as jnp
from jax import lax
import numpy as np

D_MODEL = 4096
BATCH = 2
SEQ = 4096
DEPTH = 2
DEC_BATCH = 4
DEC_SEQ = 4096
PAST_LEN = 128

HEAD_DIM = 128
ROPE_THETA = 10000.0
NORM_EPS = 1e-6
Q_BLOCK = 128
NEG_INF = -1e30

A_HEADS = 8
A_Q_RANK = 1536
A_KV_RANK = 512
A_NOPE = 128
A_ROPE = 64
A_V = 128

B_HEADS = 8
B_QK_DIM = HEAD_DIM // 2
B_V_DIM = HEAD_DIM
B_SUBLN_EPS = 1e-5

C_PATTERNS = ((128, 1), (512, 4), (2048, 16))
C_GROUPS = len(C_PATTERNS)
C_HEADS = 8

D_HEADS = 8
D_KV_HEADS = 2
D_WINDOW = 128
D_BLOCK = 128

A_COLS = A_Q_RANK + A_KV_RANK + A_ROPE
B_COLS = B_HEADS * (4 * B_QK_DIM + B_V_DIM)
C_COLS = 3 * C_GROUPS * C_HEADS * HEAD_DIM
D_COLS = (D_HEADS + 2 * D_KV_HEADS) * HEAD_DIM
IN_COLS = A_COLS + B_COLS + C_COLS + D_COLS
MIX_WIDTH = A_HEADS * A_V + B_HEADS * B_V_DIM + C_HEADS * HEAD_DIM + D_HEADS * HEAD_DIM

N_GROUPS = 8
EXPERTS_PER_GROUP = 8
N_EXPERTS = N_GROUPS * EXPERTS_PER_GROUP
EXPERT_FF = 512
TOP_K_INNER = 2
MOE_CHUNK = 128

kernel_name = "hybrid_parallel_group_encoder"


def rms_norm(x, g, eps=NORM_EPS):
    xf = x.astype(jnp.float32)
    y = xf * lax.rsqrt(jnp.mean(xf * xf, axis=-1, keepdims=True) + eps)
    return (y * g.astype(jnp.float32)).astype(x.dtype)


def rope_tables(seq, dim):
    inv = 1.0 / (ROPE_THETA ** (jnp.arange(0, dim, 2, dtype=jnp.float32) / dim))
    ang = jnp.arange(seq, dtype=jnp.float32)[:, None] * inv[None, :]
    return jnp.cos(ang), jnp.sin(ang)


def apply_rope(x, cos, sin):
    half = x.shape[-1] // 2
    x1 = x[..., :half].astype(jnp.float32)
    x2 = x[..., half:].astype(jnp.float32)
    c = cos[None, :, None, :]
    s = sin[None, :, None, :]
    return jnp.concatenate([x1 * c - x2 * s, x1 * s + x2 * c], axis=-1).astype(x.dtype)


def query_blocks(t):
    b, s = t.shape[:2]
    return jnp.moveaxis(t.reshape(b, s // Q_BLOCK, Q_BLOCK, *t.shape[2:]), 1, 0)


def merge_blocks(o):
    o = jnp.moveaxis(o, 0, 1)
    return o.reshape(o.shape[0], -1, *o.shape[3:])


def mla_mixer(cols, cos, sin, q_norm, w_uq, kv_norm, w_ukv):
    b, s, _ = cols.shape
    c_q, c_kv, k_pe = jnp.split(cols, [A_Q_RANK, A_Q_RANK + A_KV_RANK], axis=-1)
    q = (rms_norm(c_q, q_norm) @ w_uq).reshape(b, s, A_HEADS, A_NOPE + A_ROPE)
    kv = (rms_norm(c_kv, kv_norm) @ w_ukv).reshape(b, s, A_HEADS, A_NOPE + A_V)
    k_nope, v = kv[..., :A_NOPE], kv[..., A_NOPE:]
    q = jnp.concatenate([q[..., :A_NOPE], apply_rope(q[..., A_NOPE:], cos, sin)], axis=-1)
    k_pe = apply_rope(k_pe[:, :, None, :], cos, sin)
    k = jnp.concatenate([k_nope, jnp.broadcast_to(k_pe, (b, s, A_HEADS, A_ROPE))], axis=-1)
    scale = (A_NOPE + A_ROPE) ** -0.5

    def block(qb):
        sc = jnp.einsum('bqhd,bkhd->bhqk', qb, k, preferred_element_type=jnp.float32) * scale
        p = jax.nn.softmax(sc, axis=-1).astype(v.dtype)
        return jnp.einsum('bhqk,bkhd->bqhd', p, v)

    o = merge_blocks(lax.map(block, query_blocks(q)))
    return o.reshape(b, s, A_HEADS * A_V)


def diff_mixer(cols, cos, sin, lq1, lk1, lq2, lk2, subln_g, lambda_init):
    b, s, _ = cols.shape
    qk = B_HEADS * 2 * B_QK_DIM
    q, k, v = jnp.split(cols, [qk, 2 * qk], axis=-1)
    q = apply_rope(q.reshape(b, s, 2 * B_HEADS, B_QK_DIM), cos, sin).reshape(b, s, B_HEADS, 2, B_QK_DIM)
    k = apply_rope(k.reshape(b, s, 2 * B_HEADS, B_QK_DIM), cos, sin).reshape(b, s, B_HEADS, 2, B_QK_DIM)
    v = v.reshape(b, s, B_HEADS, B_V_DIM)
    f32 = jnp.float32
    lam = (jnp.exp(jnp.sum(lq1.astype(f32) * lk1.astype(f32)))
           - jnp.exp(jnp.sum(lq2.astype(f32) * lk2.astype(f32))) + lambda_init)
    scale = B_QK_DIM ** -0.5

    def block(qb):
        sc = jnp.einsum('bqhtd,bkhtd->bhtqk', qb, k, preferred_element_type=f32) * scale
        p = jax.nn.softmax(sc, axis=-1)
        a = (p[:, :, 0] - lam * p[:, :, 1]).astype(v.dtype)
        return jnp.einsum('bhqk,bkhd->bqhd', a, v)

    o = merge_blocks(lax.map(block, query_blocks(q)))
    o = rms_norm(o, subln_g, eps=B_SUBLN_EPS) * (1.0 - lambda_init)
    return o.reshape(b, s, B_HEADS * B_V_DIM)


def dilated_mixer(cols, cos, sin):
    b, s, _ = cols.shape
    nh = C_GROUPS * C_HEADS
    qkv = cols.reshape(b, s, 3, nh, HEAD_DIM)
    q = apply_rope(qkv[:, :, 0], cos, sin).reshape(b, s, C_GROUPS, C_HEADS, HEAD_DIM)
    k = apply_rope(qkv[:, :, 1], cos, sin).reshape(b, s, C_GROUPS, C_HEADS, HEAD_DIM)
    v = qkv[:, :, 2].reshape(b, s, C_GROUPS, C_HEADS, HEAD_DIM)
    scale = HEAD_DIM ** -0.5

    def block(args):
        qb, start = args
        pos = start + jnp.arange(Q_BLOCK)
        outs, maxes, dens = [], [], []
        for g, (window, dil) in enumerate(C_PATTERNS):
            n_side = window // (2 * dil)
            n_keys = 2 * n_side + 1
            idx = pos[:, None] + dil * jnp.arange(-n_side, n_side + 1)[None, :]
            valid = (idx >= 0) & (idx < s)
            flat = jnp.clip(idx, 0, s - 1).reshape(-1)
            kg = jnp.take(k[:, :, g], flat, axis=1).reshape(b, Q_BLOCK, n_keys, C_HEADS, HEAD_DIM)
            vg = jnp.take(v[:, :, g], flat, axis=1).reshape(b, Q_BLOCK, n_keys, C_HEADS, HEAD_DIM)
            sc = jnp.einsum('bqhd,bqjhd->bqhj', qb[:, :, g], kg, preferred_element_type=jnp.float32) * scale
            sc = jnp.where(valid[None, :, None, :], sc, NEG_INF)
            m = jnp.max(sc, axis=-1, keepdims=True)
            p = jnp.exp(sc - m)
            l = jnp.sum(p, axis=-1, keepdims=True)
            outs.append(jnp.einsum('bqhj,bqjhd->bqhd', (p / l).astype(vg.dtype), vg,
                                   preferred_element_type=jnp.float32))
            maxes.append(m)
            dens.append(l)
        m_all = jnp.stack(maxes)
        l_all = jnp.stack(dens)
        wts = l_all * jnp.exp(m_all - jnp.max(m_all, axis=0, keepdims=True))
        wts = wts / jnp.sum(wts, axis=0, keepdims=True)
        return jnp.sum(wts * jnp.stack(outs), axis=0).astype(cols.dtype)

    starts = jnp.arange(s // Q_BLOCK, dtype=jnp.int32) * Q_BLOCK
    o = merge_blocks(lax.map(block, (query_blocks(q), starts)))
    return o.reshape(b, s, C_HEADS * HEAD_DIM)


def window_gqa_mixer(cols, cos, sin, sink):
    b, s, _ = cols.shape
    q, k, v = jnp.split(cols, [D_HEADS * HEAD_DIM, (D_HEADS + D_KV_HEADS) * HEAD_DIM], axis=-1)
    q = apply_rope(q.reshape(b, s, D_HEADS, HEAD_DIM), cos, sin)
    k = apply_rope(k.reshape(b, s, D_KV_HEADS, HEAD_DIM), cos, sin)
    v = v.reshape(b, s, D_KV_HEADS, HEAD_DIM)
    nb = s // D_BLOCK
    grp = D_HEADS // D_KV_HEADS

    def banded(t):
        tp = jnp.pad(t, ((0, 0), (D_BLOCK, D_BLOCK), (0, 0), (0, 0)))
        tb = tp.reshape(b, nb + 2, D_BLOCK, *t.shape[2:])
        return jnp.concatenate([tb[:, :-2], tb[:, 1:-1], tb[:, 2:]], axis=2)

    kb, vb = banded(k), banded(v)
    qb = q.reshape(b, nb, D_BLOCK, D_KV_HEADS, grp, HEAD_DIM)
    sc = jnp.einsum('bnqkgd,bnjkd->bnkgqj', qb, kb, preferred_element_type=jnp.float32) * (HEAD_DIM ** -0.5)
    qpos = jnp.arange(nb)[:, None] * D_BLOCK + jnp.arange(D_BLOCK)[None, :]
    kpos = jnp.arange(nb)[:, None] * D_BLOCK - D_BLOCK + jnp.arange(3 * D_BLOCK)[None, :]
    valid = ((jnp.abs(kpos[:, None, :] - qpos[:, :, None]) <= D_WINDOW)
             & (kpos[:, None, :] >= 0) & (kpos[:, None, :] < s))
    sc = jnp.where(valid[None, :, None, None], sc, NEG_INF)
    sink_l = sink.astype(jnp.float32).reshape(D_KV_HEADS, grp)[None, None, :, :, None, None]
    m = jnp.maximum(jnp.max(sc, axis=-1, keepdims=True), sink_l)
    p = jnp.exp(sc - m)
    den = jnp.sum(p, axis=-1, keepdims=True) + jnp.exp(sink_l - m)
    o = jnp.einsum('bnkgqj,bnjkd->bnqkgd', (p / den).astype(v.dtype), vb)
    return o.reshape(b, s, D_HEADS * HEAD_DIM)


def hier_moe(h, w_rg, b_rg, w_re, b_re, w_gate, w_up, w_down):
    b, s, d = h.shape
    t = h.reshape(-1, d)
    f32 = jnp.float32
    g_logits = jnp.einsum('td,dg->tg', t, w_rg, preferred_element_type=f32) + b_rg.astype(f32)
    g_prob = jax.nn.softmax(g_logits, axis=-1)
    g_idx = jnp.argmax(g_logits, axis=-1)
    g_w = jnp.take_along_axis(g_prob, g_idx[:, None], axis=-1)
    e_logits = jnp.einsum('td,dge->tge', t, w_re, preferred_element_type=f32) + b_re.astype(f32)
    e_sel = jnp.take_along_axis(e_logits, g_idx[:, None, None], axis=1)[:, 0]
    top_v, top_i = lax.top_k(e_sel, TOP_K_INNER)
    top_w = jax.nn.softmax(top_v, axis=-1)
    e_w = jnp.sum(jax.nn.one_hot(top_i, EXPERTS_PER_GROUP, dtype=f32) * top_w[..., None], axis=1)
    gate = (jax.nn.one_hot(g_idx, N_GROUPS, dtype=f32)[:, :, None]
            * (g_w[:, :, None] * e_w[:, None, :])).reshape(-1, N_EXPERTS)

    def chunk(args):
        xc, gc = args
        hg = jnp.einsum('cd,edf->cef', xc, w_gate)
        hu = jnp.einsum('cd,edf->cef', xc, w_up)
        a = jax.nn.silu(hg) * hu * gc.astype(xc.dtype)[:, :, None]
        return jnp.einsum('cef,efd->cd', a, w_down)

    nc = t.shape[0] // MOE_CHUNK
    y = lax.map(chunk, (t.reshape(nc, MOE_CHUNK, d), gate.reshape(nc, MOE_CHUNK, N_EXPERTS)))
    return y.reshape(b, s, d)


def trunk(x, attn_norm, w_in, mla_q_norm, mla_w_uq, mla_kv_norm, mla_w_ukv,
          diff_lq1, diff_lk1, diff_lq2, diff_lk2, diff_subln, win_sink, w_o,
          ffn_norm, router_group_w, router_group_b, router_expert_w, router_expert_b,
          expert_w_gate, expert_w_up, expert_w_down, final_norm):
    s = x.shape[1]
    cos_a, sin_a = rope_tables(s, A_ROPE)
    cos_b, sin_b = rope_tables(s, B_QK_DIM)
    cos_h, sin_h = rope_tables(s, HEAD_DIM)
    splits = [A_COLS, A_COLS + B_COLS, A_COLS + B_COLS + C_COLS]
    for i in range(DEPTH):
        lambda_init = 0.8 - 0.6 * math.exp(-0.3 * i)
        h = rms_norm(x, attn_norm[i])
        cols = h @ w_in[i]
        ca, cb, cc, cd = jnp.split(cols, splits, axis=-1)
        o = jnp.concatenate([
            mla_mixer(ca, cos_a, sin_a, mla_q_norm[i], mla_w_uq[i], mla_kv_norm[i], mla_w_ukv[i]),
            diff_mixer(cb, cos_b, sin_b, diff_lq1[i], diff_lk1[i], diff_lq2[i], diff_lk2[i],
                       diff_subln[i], lambda_init),
            dilated_mixer(cc, cos_h, sin_h),
            window_gqa_mixer(cd, cos_h, sin_h, win_sink[i]),
        ], axis=-1)
        x = x + o @ w_o[i]
        x = x + hier_moe(rms_norm(x, ffn_norm[i]), router_group_w[i], router_group_b[i],
                         router_expert_w[i], router_expert_b[i],
                         expert_w_gate[i], expert_w_up[i], expert_w_down[i])
    return rms_norm(x, final_norm)


def setup_inputs(seed: int = 0) -> dict:
    key = jax.random.key(seed)
    ks = jax.random.split(key, 24)
    f32 = jnp.float32

    def nrm(k, shape, scale):
        return jax.random.normal(k, shape, f32) * scale

    def gain(k, shape):
        return 1.0 + 0.02 * jax.random.normal(k, shape, f32)

    return {
        "x_prompt": nrm(ks[0], (BATCH, SEQ, D_MODEL), 1.0),
        "x_sample": nrm(ks[1], (DEC_BATCH, DEC_SEQ, D_MODEL), 1.0),
        "attn_norm": gain(ks[2], (DEPTH, D_MODEL)),
        "w_in": nrm(ks[3], (DEPTH, D_MODEL, IN_COLS), D_MODEL ** -0.5),
        "mla_q_norm": gain(ks[4], (DEPTH, A_Q_RANK)),
        "mla_w_uq": nrm(ks[5], (DEPTH, A_Q_RANK, A_HEADS * (A_NOPE + A_ROPE)), A_Q_RANK ** -0.5),
        "mla_kv_norm": gain(ks[6], (DEPTH, A_KV_RANK)),
        "mla_w_ukv": nrm(ks[7], (DEPTH, A_KV_RANK, A_HEADS * (A_NOPE + A_V)), A_KV_RANK ** -0.5),
        "diff_lq1": nrm(ks[8], (DEPTH, B_QK_DIM), 0.1),
        "diff_lk1": nrm(ks[9], (DEPTH, B_QK_DIM), 0.1),
        "diff_lq2": nrm(ks[10], (DEPTH, B_QK_DIM), 0.1),
        "diff_lk2": nrm(ks[11], (DEPTH, B_QK_DIM), 0.1),
        "diff_subln": gain(ks[12], (DEPTH, B_V_DIM)),
        "win_sink": nrm(ks[13], (DEPTH, D_HEADS), 0.5),
        "w_o": nrm(ks[14], (DEPTH, MIX_WIDTH, D_MODEL), MIX_WIDTH ** -0.5),
        "ffn_norm": gain(ks[15], (DEPTH, D_MODEL)),
        "router_group_w": nrm(ks[16], (DEPTH, D_MODEL, N_GROUPS), D_MODEL ** -0.5),
        "router_group_b": nrm(ks[17], (DEPTH, N_GROUPS), 0.01),
        "router_expert_w": nrm(ks[18], (DEPTH, D_MODEL, N_GROUPS, EXPERTS_PER_GROUP), D_MODEL ** -0.5),
        "router_expert_b": nrm(ks[19], (DEPTH, N_GROUPS, EXPERTS_PER_GROUP), 0.01),
        "expert_w_gate": nrm(ks[20], (DEPTH, N_EXPERTS, D_MODEL, EXPERT_FF), D_MODEL ** -0.5),
        "expert_w_up": nrm(ks[21], (DEPTH, N_EXPERTS, D_MODEL, EXPERT_FF), D_MODEL ** -0.5),
        "expert_w_down": nrm(ks[22], (DEPTH, N_EXPERTS, EXPERT_FF, D_MODEL), EXPERT_FF ** -0.5),
        "final_norm": gain(ks[23], (D_MODEL,)),
    }


def reference(x_prompt, x_sample, attn_norm, w_in, mla_q_norm, mla_w_uq, mla_kv_norm, mla_w_ukv,
              diff_lq1, diff_lk1, diff_lq2, diff_lk2, diff_subln, win_sink, w_o,
              ffn_norm, router_group_w, router_group_b, router_expert_w, router_expert_b,
              expert_w_gate, expert_w_up, expert_w_down, final_norm):
    y_prompt = trunk(x_prompt, attn_norm, w_in, mla_q_norm, mla_w_uq, mla_kv_norm, mla_w_ukv,
                     diff_lq1, diff_lk1, diff_lq2, diff_lk2, diff_subln, win_sink, w_o,
                     ffn_norm, router_group_w, router_group_b, router_expert_w, router_expert_b,
                     expert_w_gate, expert_w_up, expert_w_down, final_norm)
    y_sample = trunk(x_sample, attn_norm, w_in, mla_q_norm, mla_w_uq, mla_kv_norm, mla_w_ukv,
                     diff_lq1, diff_lk1, diff_lq2, diff_lk2, diff_subln, win_sink, w_o,
                     ffn_norm, router_group_w, router_group_b, router_expert_w, router_expert_b,
                     expert_w_gate, expert_w_up, expert_w_down, final_norm)
    return (y_prompt, y_sample)
```

```python
import functools
import math

import jax
import jax.numpy as jnp
from jax import lax
from jax.experimental import pallas as pl
from jax.experimental.pallas import tpu as pltpu

F32 = jnp.float32
BF16 = jnp.bfloat16

HEAD_DIM = 128
ROPE_THETA = 10000.0
NORM_EPS = 1e-6
NEG_INF = -1e30

A_HEADS = 8
A_Q_RANK = 1536
A_KV_RANK = 512
A_NOPE = 128
A_ROPE = 64
A_V = 128
A_QK_PAD = 256

B_HEADS = 8
B_QK_DIM = 64
B_V_DIM = 128
B_SUBLN_EPS = 1e-5

C_PATTERNS = ((128, 1), (512, 4), (2048, 16))
C_GROUPS = len(C_PATTERNS)
C_HEADS = 8

D_HEADS = 8
D_KV_HEADS = 2
D_WINDOW = 128

N_GROUPS = 8
EXPERTS_PER_GROUP = 8
N_EXPERTS = N_GROUPS * EXPERTS_PER_GROUP
EXPERT_FF = 512

LANES = 128
VMEM_LIMIT = 56 * 1024 * 1024

N_CQ, N_CKV, N_KPE = 0, A_Q_RANK, A_Q_RANK + A_KV_RANK
N_BV = N_KPE + LANES
N_CV = N_BV + B_HEADS * B_V_DIM
N_DV = N_CV + C_GROUPS * C_HEADS * HEAD_DIM
N_COLS = N_DV + D_KV_HEADS * HEAD_DIM
R64_BQ, R64_BK = 0, B_HEADS * 2 * B_QK_DIM
R64_COLS = 2 * R64_BK
R128_CQ = 0
R128_CK = C_GROUPS * C_HEADS * HEAD_DIM
R128_DQ = 2 * R128_CK
R128_DK = R128_DQ + D_HEADS * HEAD_DIM
R128_COLS = R128_DK + D_KV_HEADS * HEAD_DIM


def _round_up(x, m):
    return (x + m - 1) // m * m


def _tile(n, pref):
    if n <= pref:
        return n
    t = pref
    while n % t:
        t //= 2
    return t


def _params(*sem):
    return pltpu.CompilerParams(dimension_semantics=sem, vmem_limit_bytes=VMEM_LIMIT)


def _dot_nt(a, b):
    return lax.dot_general(a, b, (((1,), (1,)), ((), ())), preferred_element_type=F32)


def _rope_tile(x, cos, s_up, s_dn, half):
    if 2 * half == LANES:
        return x * cos + pltpu.roll(x, half, 1) * s_up
    return (x * cos + pltpu.roll(x, LANES - half, 1) * s_up
            + pltpu.roll(x, half, 1) * s_dn)


def _rmsnorm_kernel(x_ref, g_ref, o_ref):
    x = x_ref[...]
    ms = jnp.mean(x * x, axis=-1, keepdims=True)
    o_ref[...] = (x * lax.rsqrt(ms + NORM_EPS) * g_ref[...]).astype(o_ref.dtype)


def rmsnorm_bf16(x, g):
    t, d = x.shape
    tm = _tile(t, 512)
    return pl.pallas_call(
        _rmsnorm_kernel,
        grid=(t // tm,),
        in_specs=[pl.BlockSpec((tm, d), lambda i: (i, 0)),
                  pl.BlockSpec((1, d), lambda i: (0, 0))],
        out_specs=pl.BlockSpec((tm, d), lambda i: (i, 0)),
        out_shape=jax.ShapeDtypeStruct((t, d), BF16),
        compiler_params=_params("parallel"),
    )(x, g.reshape(1, d))


def _proj_kernel(*refs, half):
    if half is None:
        h_ref, w_ref, o_ref = refs
        o_ref[...] = jnp.dot(h_ref[...], w_ref[...],
                             preferred_element_type=F32).astype(o_ref.dtype)
        return
    h_ref, w_ref, cs_ref, cos_ref, sup_ref, sdn_ref, o_ref = refs
    acc = jnp.dot(h_ref[...], w_ref[...], preferred_element_type=F32) * cs_ref[...]
    cos, s_up, s_dn = cos_ref[...], sup_ref[...], sdn_ref[...]
    for c in range(acc.shape[1] // LANES):
        sl = slice(c * LANES, (c + 1) * LANES)
        o_ref[:, sl] = _rope_tile(acc[:, sl], cos, s_up, s_dn, half).astype(o_ref.dtype)


def in_projection(h, w, seq, rope=None):
    t, d = h.shape
    n = w.shape[1]
    tm = _tile(seq, 1024)
    tn = _tile(n, 512)
    pos_blocks = seq // tm
    in_specs = [pl.BlockSpec((tm, d), lambda i, j: (i, 0)),
                pl.BlockSpec((d, tn), lambda i, j: (0, j))]
    args = [h, w]
    half = None
    if rope is not None:
        half, colscale, cos, s_up, s_dn = rope
        in_specs.append(pl.BlockSpec((1, tn), lambda i, j: (0, j)))
        tab = pl.BlockSpec((tm, LANES), lambda i, j: (i % pos_blocks, 0))
        in_specs += [tab, tab, tab]
        args += [colscale, cos, s_up, s_dn]
    return pl.pallas_call(
        functools.partial(_proj_kernel, half=half),
        grid=(t // tm, n // tn),
        in_specs=in_specs,
        out_specs=pl.BlockSpec((tm, tn), lambda i, j: (i, j)),
        out_shape=jax.ShapeDtypeStruct((t, n), BF16),
        compiler_params=_params("parallel", "parallel"),
    )(*args)


def _mla_proj_kernel(cq_ref, ckv_ref, kpe_ref, qn_ref, kvn_ref, wuq_ref, wukv_ref,
                     cos_ref, sup_ref, sdn_ref, q_ref, k_ref, v_ref, *, scale):
    cos, s_up, s_dn = cos_ref[...], sup_ref[...], sdn_ref[...]

    def normed(ref, g_ref):
        x = ref[...].astype(F32)
        ms = jnp.mean(x * x, axis=-1, keepdims=True)
        return (x * lax.rsqrt(ms + NORM_EPS) * g_ref[...]).astype(BF16)

    q = jnp.dot(normed(cq_ref, qn_ref), wuq_ref[...], preferred_element_type=F32) * scale
    kv = jnp.dot(normed(ckv_ref, kvn_ref), wukv_ref[...], preferred_element_type=F32)
    kpe = _rope_tile(kpe_ref[...].astype(F32), cos, s_up, s_dn, A_ROPE // 2).astype(BF16)
    for h in range(A_HEADS):
        c0 = h * A_QK_PAD
        q_ref[:, c0:c0 + A_NOPE] = q[:, c0:c0 + A_NOPE].astype(BF16)
        q_ref[:, c0 + A_NOPE:c0 + A_QK_PAD] = _rope_tile(
            q[:, c0 + A_NOPE:c0 + A_QK_PAD], cos, s_up, s_dn, A_ROPE // 2).astype(BF16)
        k_ref[:, c0:c0 + A_NOPE] = kv[:, h * A_NOPE:(h + 1) * A_NOPE].astype(BF16)
        k_ref[:, c0 + A_NOPE:c0 + A_QK_PAD] = kpe
    v_ref[...] = kv[:, A_HEADS * A_NOPE:].astype(BF16)


def mla_projection(cols_n, q_norm, kv_norm, w_uq, w_ukv, tabs64, seq):
    t = cols_n.shape[0]
    tm = _tile(seq, 512)
    pos_blocks = seq // tm
    qw = A_HEADS * A_QK_PAD
    vw = A_HEADS * A_V
    tab = pl.BlockSpec((tm, LANES), lambda i: (i % pos_blocks, 0))
    full = lambda shape: pl.BlockSpec(shape, lambda i: (0, 0))
    return pl.pallas_call(
        functools.partial(_mla_proj_kernel, scale=(A_NOPE + A_ROPE) ** -0.5),
        grid=(t // tm,),
        in_specs=[pl.BlockSpec((tm, A_Q_RANK), lambda i: (i, N_CQ // A_Q_RANK)),
                  pl.BlockSpec((tm, A_KV_RANK), lambda i: (i, N_CKV // A_KV_RANK)),
                  pl.BlockSpec((tm, LANES), lambda i: (i, N_KPE // LANES)),
                  full((1, A_Q_RANK)), full((1, A_KV_RANK)),
                  full((A_Q_RANK, qw)), full((A_KV_RANK, qw)),
                  tab, tab, tab],
        out_specs=[pl.BlockSpec((tm, qw), lambda i: (i, 0)),
                   pl.BlockSpec((tm, qw), lambda i: (i, 0)),
                   pl.BlockSpec((tm, vw), lambda i: (i, 0))],
        out_shape=[jax.ShapeDtypeStruct((t, qw), BF16),
                   jax.ShapeDtypeStruct((t, qw), BF16),
                   jax.ShapeDtypeStruct((t, vw), BF16)],
        compiler_params=_params("parallel"),
    )(cols_n, cols_n, cols_n, q_norm.reshape(1, -1), kv_norm.reshape(1, -1),
      w_uq, w_ukv, *tabs64)


def _softmax_parts(s):
    m = jnp.max(s, axis=-1, keepdims=True)
    e = jnp.exp(s - m)
    return e, jnp.sum(e, axis=-1, keepdims=True)


def _full_attn_kernel(q_ref, k_ref, v_ref, o_ref):
    e, l = _softmax_parts(_dot_nt(q_ref[...], k_ref[...]))
    o = jnp.dot(e.astype(BF16), v_ref[...], preferred_element_type=F32)
    o_ref[...] = (o / l).astype(o_ref.dtype)


def mla_attention(q, k, v, nseq, seq):
    tq = _tile(seq, 256)
    nq = seq // tq
    return pl.pallas_call(
        _full_attn_kernel,
        grid=(nseq, A_HEADS, nq),
        in_specs=[pl.BlockSpec((tq, A_QK_PAD), lambda b, h, i: (b * nq + i, h)),
                  pl.BlockSpec((seq, A_QK_PAD), lambda b, h, i: (b, h)),
                  pl.BlockSpec((seq, A_V), lambda b, h, i: (b, h))],
        out_specs=pl.BlockSpec((tq, A_V), lambda b, h, i: (b * nq + i, h)),
        out_shape=jax.ShapeDtypeStruct((nseq * seq, A_HEADS * A_V), BF16),
        compiler_params=_params("parallel", "parallel", "parallel"),
    )(q, k, v)


def _diff_attn_kernel(q_ref, k_ref, v_ref, lv_ref, g_ref, o_ref, *, lambda_init):
    lv = lv_ref[...]
    lam = (jnp.exp(jnp.sum(lv[0:1] * lv[1:2], axis=-1, keepdims=True))
           - jnp.exp(jnp.sum(lv[2:3] * lv[3:4], axis=-1, keepdims=True)) + lambda_init)
    q = q_ref[...]
    k = k_ref[...]
    v = v_ref[...]
    lane = lax.broadcasted_iota(jnp.int32, (1, LANES), 1)
    zero = jnp.zeros_like(q)
    e0, l0 = _softmax_parts(_dot_nt(jnp.where(lane < B_QK_DIM, q, zero), k))
    e1, l1 = _softmax_parts(_dot_nt(jnp.where(lane >= B_QK_DIM, q, zero), k))
    o0 = jnp.dot(e0.astype(BF16), v, preferred_element_type=F32)
    o1 = jnp.dot(e1.astype(BF16), v, preferred_element_type=F32)
    o = o0 / l0 - lam * (o1 / l1)
    ms = jnp.mean(o * o, axis=-1, keepdims=True)
    o = o * lax.rsqrt(ms + B_SUBLN_EPS) * g_ref[...] * (1.0 - lambda_init)
    o_ref[...] = o.astype(o_ref.dtype)


def diff_attention(cols_r64, cols_n, lvec, subln_g, lambda_init, nseq, seq):
    tq = _tile(seq, 256)
    nq = seq // tq
    qb, kb, vb = R64_BQ // LANES, R64_BK // LANES, N_BV // LANES
    return pl.pallas_call(
        functools.partial(_diff_attn_kernel, lambda_init=lambda_init),
        grid=(nseq, B_HEADS, nq),
        in_specs=[pl.BlockSpec((tq, LANES), lambda b, h, i: (b * nq + i, qb + h)),
                  pl.BlockSpec((seq, LANES), lambda b, h, i: (b, kb + h)),
                  pl.BlockSpec((seq, B_V_DIM), lambda b, h, i: (b, vb + h)),
                  pl.BlockSpec((8, LANES), lambda b, h, i: (0, 0)),
                  pl.BlockSpec((1, B_V_DIM), lambda b, h, i: (0, 0))],
        out_specs=pl.BlockSpec((tq, B_V_DIM), lambda b, h, i: (b * nq + i, h)),
        out_shape=jax.ShapeDtypeStruct((nseq * seq, B_HEADS * B_V_DIM), BF16),
        compiler_params=_params("parallel", "parallel", "parallel"),
    )(cols_r64, cols_r64, cols_n, lvec, subln_g.reshape(1, -1))


def _window_attn_kernel(*refs, length, half_window, has_sink, want_stat):
    refs = list(refs)
    sink_ref = refs.pop(0) if has_sink else None
    q_ref, k_ref, v_ref, o_ref = refs[:4]
    stat_ref = refs[4] if want_stat else None
    h = pl.program_id(1)
    qb = min(LANES, length)
    kw = min(qb + 2 * half_window, length)
    lane = lax.broadcasted_iota(jnp.int32, (1, LANES), 1)

    if want_stat:
        @pl.when(h == 0)
        def _():
            stat_ref[...] = jnp.zeros_like(stat_ref)

    def body(i, carry):
        r0 = pl.multiple_of(i * qb, qb)
        ks = pl.multiple_of(jnp.clip(r0 - half_window, 0, length - kw), 64)
        s = _dot_nt(q_ref[pl.ds(r0, qb), :], k_ref[pl.ds(ks, kw), :])
        qpos = r0 + lax.broadcasted_iota(jnp.int32, (qb, 1), 0)
        kpos = ks + lax.broadcasted_iota(jnp.int32, (1, kw), 1)
        s = jnp.where(jnp.abs(kpos - qpos) <= half_window, s, NEG_INF)
        m = jnp.max(s, axis=-1, keepdims=True)
        if has_sink:
            sink = sink_ref[h]
            m = jnp.maximum(m, sink)
        e = jnp.exp(s - m)
        l = jnp.sum(e, axis=-1, keepdims=True)
        if has_sink:
            l = l + jnp.exp(sink - m)
        o = jnp.dot(e.astype(BF16), v_ref[pl.ds(ks, kw), :], preferred_element_type=F32)
        o_ref[pl.ds(r0, qb), :] = (o / l).astype(o_ref.dtype)
        if want_stat:
            lse = m + jnp.log(l)
            stat_ref[pl.ds(r0, qb), :] = jnp.where(lane == h, lse, stat_ref[pl.ds(r0, qb), :])
        return carry

    lax.fori_loop(0, length // qb, body, 0)


def window_attention(q_arr, q_blk, k_arr, k_blk, v_arr, v_blk, *, nseq, length,
                     heads, kv_group, half_window, sink=None, want_stat=False):
    spec = lambda blk, grp: pl.BlockSpec((length, HEAD_DIM),
                                         lambda b, h: (b, blk + h // grp))
    in_specs = [spec(q_blk, 1), spec(k_blk, kv_group), spec(v_blk, kv_group)]
    args = [q_arr, k_arr, v_arr]
    if sink is not None:
        in_specs.insert(0, pl.BlockSpec(memory_space=pltpu.SMEM))
        args.insert(0, sink.astype(F32))
    out_specs = [pl.BlockSpec((length, HEAD_DIM), lambda b, h: (b, h))]
    out_shape = [jax.ShapeDtypeStruct((nseq * length, heads * HEAD_DIM), BF16)]
    if want_stat:
        out_specs.append(pl.BlockSpec((length, LANES), lambda b, h: (b, 0)))
        out_shape.append(jax.ShapeDtypeStruct((nseq * length, LANES), F32))
    return pl.pallas_call(
        functools.partial(_window_attn_kernel, length=length, half_window=half_window,
                          has_sink=sink is not None, want_stat=want_stat),
        grid=(nseq, heads),
        in_specs=in_specs,
        out_specs=out_specs,
        out_shape=out_shape,
        compiler_params=_params("parallel", "arbitrary"),
    )(*args)


def _merge_kernel(*refs):
    o_refs, s_refs, out_ref = refs[:C_GROUPS], refs[C_GROUPS:2 * C_GROUPS], refs[-1]
    lse = [r[...] for r in s_refs]
    m = functools.reduce(jnp.maximum, lse)
    w = [jnp.exp(x - m) for x in lse]
    inv = 1.0 / functools.reduce(lambda a, b: a + b, w)
    w = [x * inv for x in w]
    for h in range(C_HEADS):
        sl = slice(h * HEAD_DIM, (h + 1) * HEAD_DIM)
        acc = w[0][:, h:h + 1] * o_refs[0][:, sl].astype(F32)
        for g in range(1, C_GROUPS):
            acc = acc + w[g][:, h:h + 1] * o_refs[g][:, sl].astype(F32)
        out_ref[:, sl] = acc.astype(out_ref.dtype)


def merge_groups(outs, stats):
    t, w = outs[0].shape
    tm = _tile(t, 512)
    return pl.pallas_call(
        _merge_kernel,
        grid=(t // tm,),
        in_specs=([pl.BlockSpec((tm, w), lambda i: (i, 0))] * C_GROUPS
                  + [pl.BlockSpec((tm, LANES), lambda i: (i, 0))] * C_GROUPS),
        out_specs=pl.BlockSpec((tm, w), lambda i: (i, 0)),
        out_shape=jax.ShapeDtypeStruct((t, w), BF16),
        compiler_params=_params("parallel"),
    )(*outs, *stats)


def _out_proj_kernel(a_ref, b_ref, c_ref, d_ref, w_ref, x_ref, o_ref):
    o = jnp.concatenate([a_ref[...], b_ref[...], c_ref[...], d_ref[...]], axis=-1)
    o_ref[...] = x_ref[...] + jnp.dot(o, w_ref[...], preferred_element_type=F32)


def out_projection(parts, w_o, x):
    t, d = x.shape
    tm = _tile(t, 1024)
    tn = _tile(d, 512)
    in_specs = [pl.BlockSpec((tm, p.shape[1]), lambda i, j: (i, 0)) for p in parts]
    in_specs += [pl.BlockSpec((w_o.shape[0], tn), lambda i, j: (0, j)),
                 pl.BlockSpec((tm, tn), lambda i, j: (i, j))]
    return pl.pallas_call(
        _out_proj_kernel,
        grid=(t // tm, d // tn),
        in_specs=in_specs,
        out_specs=pl.BlockSpec((tm, tn), lambda i, j: (i, j)),
        out_shape=jax.ShapeDtypeStruct((t, d), F32),
        compiler_params=_params("parallel", "parallel"),
    )(*parts, w_o, x)


def _router_kernel(x_ref, g_ref, whi_ref, wlo_ref, b_ref, h_ref, r_ref):
    x = x_ref[...]
    ms = jnp.mean(x * x, axis=-1, keepdims=True)
    hn = x * lax.rsqrt(ms + NORM_EPS) * g_ref[...]
    hi = hn.astype(BF16)
    h_ref[...] = hi
    lo = (hn - hi.astype(F32)).astype(BF16)
    w_hi = whi_ref[...]
    logits = (jnp.dot(hi, w_hi, preferred_element_type=F32)
              + jnp.dot(lo, w_hi, preferred_element_type=F32)
              + jnp.dot(hi, wlo_ref[...], preferred_element_type=F32)) + b_ref[...]
    lane = lax.broadcasted_iota(jnp.int32, logits.shape, 1)
    ninf = float("-inf")
    big = jnp.int32(LANES)
    gmask = (lane >= N_EXPERTS) & (lane < N_EXPERTS + N_GROUPS)
    gl = jnp.where(gmask, logits, ninf)
    gmax = jnp.max(gl, axis=-1, keepdims=True)
    g_idx = jnp.min(jnp.where(gl == gmax, lane - N_EXPERTS, big), axis=-1, keepdims=True)
    g_w = 1.0 / jnp.sum(jnp.exp(gl - gmax), axis=-1, keepdims=True)
    emask = (lane < N_EXPERTS) & ((lane // EXPERTS_PER_GROUP) == g_idx)
    el = jnp.where(emask, logits, ninf)
    v1 = jnp.max(el, axis=-1, keepdims=True)
    i1 = jnp.min(jnp.where(el == v1, lane, big), axis=-1, keepdims=True)
    el2 = jnp.where(lane == i1, ninf, el)
    v2 = jnp.max(el2, axis=-1, keepdims=True)
    i2 = jnp.min(jnp.where(el2 == v2, lane, big), axis=-1, keepdims=True)
    t = jnp.exp(v2 - v1)
    w1 = g_w / (1.0 + t)
    w2 = w1 * t
    r_ref[...] = jnp.where(lane == 0, i1.astype(F32),
                           jnp.where(lane == 1, i2.astype(F32),
                                     jnp.where(lane == 2, w1, jnp.where(lane == 3, w2, 0.0))))


def norm_and_route(x, g, w_hi, w_lo, bias):
    t, d = x.shape
    tm = _tile(t, 512)
    full = lambda shape: pl.BlockSpec(shape, lambda i: (0, 0))
    return pl.pallas_call(
        _router_kernel,
        grid=(t // tm,),
        in_specs=[pl.BlockSpec((tm, d), lambda i: (i, 0)), full((1, d)),
                  full((d, LANES)), full((d, LANES)), full((1, LANES))],
        out_specs=[pl.BlockSpec((tm, d), lambda i: (i, 0)),
                   pl.BlockSpec((tm, LANES), lambda i: (i, 0))],
        out_shape=[jax.ShapeDtypeStruct((t, d), BF16),
                   jax.ShapeDtypeStruct((t, LANES), F32)],
        compiler_params=_params("parallel"),
    )(x, g.reshape(1, d), w_hi, w_lo, bias)


def _expert_kernel(te_ref, tv_ref, x_ref, gate_ref, wg_ref, wu_ref, wd_ref, o_ref):
    j = pl.program_id(0)

    @pl.when(tv_ref[j] != 0)
    def _():
        x = x_ref[...]
        hg = jnp.dot(x, wg_ref[...], preferred_element_type=F32)
        hu = jnp.dot(x, wu_ref[...], preferred_element_type=F32)
        a = hg / (1.0 + jnp.exp(-hg)) * hu * gate_ref[...]
        o_ref[...] = jnp.dot(a.astype(BF16), wd_ref[...], preferred_element_type=F32)

    @pl.when(tv_ref[j] == 0)
    def _():
        o_ref[...] = jnp.zeros_like(o_ref)


def expert_ffn(xg, gate_rows, tile_expert, tile_valid, w_gate, w_up, w_down, tm):
    r, d = xg.shape
    ff = w_gate.shape[2]
    grid_spec = pltpu.PrefetchScalarGridSpec(
        num_scalar_prefetch=2,
        grid=(r // tm,),
        in_specs=[pl.BlockSpec((tm, d), lambda j, te, tv: (j, 0)),
                  pl.BlockSpec((tm, 1), lambda j, te, tv: (j, 0)),
                  pl.BlockSpec((None, d, ff), lambda j, te, tv: (te[j], 0, 0)),
                  pl.BlockSpec((None, d, ff), lambda j, te, tv: (te[j], 0, 0)),
                  pl.BlockSpec((None, ff, d), lambda j, te, tv: (te[j], 0, 0))],
        out_specs=pl.BlockSpec((tm, d), lambda j, te, tv: (j, 0)),
    )
    return pl.pallas_call(
        _expert_kernel,
        grid_spec=grid_spec,
        out_shape=jax.ShapeDtypeStruct((r, d), F32),
        compiler_params=_params("arbitrary"),
    )(tile_expert, tile_valid, xg, gate_rows, w_gate, w_up, w_down)


def _combine_kernel(x_ref, y1_ref, y2_ref, g_ref, o_ref, *, final):
    x = x_ref[...] + y1_ref[...] + y2_ref[...]
    if final:
        ms = jnp.mean(x * x, axis=-1, keepdims=True)
        x = x * lax.rsqrt(ms + NORM_EPS) * g_ref[...]
    o_ref[...] = x


def combine(x, y1, y2, g, final):
    t, d = x.shape
    tm = _tile(t, 256)
    row = pl.BlockSpec((tm, d), lambda i: (i, 0))
    return pl.pallas_call(
        functools.partial(_combine_kernel, final=final),
        grid=(t // tm,),
        in_specs=[row, row, row, pl.BlockSpec((1, d), lambda i: (0, 0))],
        out_specs=row,
        out_shape=jax.ShapeDtypeStruct((t, d), F32),
        compiler_params=_params("parallel"),
    )(x, y1, y2, g.reshape(1, d))


def dispatch_plan(route, tm):
    t = route.shape[0]
    e_flat = jnp.concatenate([route[:, 0], route[:, 1]]).astype(jnp.int32)
    g_flat = jnp.concatenate([route[:, 2], route[:, 3]])
    tok = jnp.tile(jnp.arange(t, dtype=jnp.int32), 2)
    order = jnp.argsort(e_flat, stable=True)
    e_sorted = e_flat[order]
    counts = jnp.bincount(e_flat, length=N_EXPERTS).astype(jnp.int32)
    padded = (counts + tm - 1) // tm * tm
    pad_end = jnp.cumsum(padded)
    pad_start = pad_end - padded
    start = jnp.cumsum(counts) - counts
    dest = pad_start[e_sorted] + jnp.arange(2 * t, dtype=jnp.int32) - start[e_sorted]
    rows = 2 * t + N_EXPERTS * tm
    src_token = jnp.zeros((rows,), jnp.int32).at[dest].set(tok[order])
    gate_rows = jnp.zeros((rows,), F32).at[dest].set(g_flat[order])
    tile_start = jnp.arange(rows // tm, dtype=jnp.int32) * tm
    tile_valid = (tile_start < pad_end[-1]).astype(jnp.int32)
    tile_expert = jnp.minimum(jnp.searchsorted(pad_end, tile_start, side="right"),
                              N_EXPERTS - 1).astype(jnp.int32)
    pos = jnp.zeros((2 * t,), jnp.int32).at[order].set(dest)
    return src_token, gate_rows.reshape(rows, 1), tile_expert, tile_valid, pos[:t], pos[t:]


def rope_tables(seq, dim):
    half = dim // 2
    inv = 1.0 / (ROPE_THETA ** (jnp.arange(0, dim, 2, dtype=F32) / dim))
    ang = jnp.arange(seq, dtype=F32)[:, None] * inv[None, :]
    cos, sin = jnp.cos(ang), jnp.sin(ang)
    reps = LANES // dim
    zero = jnp.zeros_like(sin)
    cos_t = jnp.tile(jnp.concatenate([cos, cos], axis=1), (1, reps))
    if reps == 1:
        return cos_t, jnp.concatenate([-sin, sin], axis=1), zero
    s_up = jnp.tile(jnp.concatenate([-sin, zero], axis=1), (1, reps))
    s_dn = jnp.tile(jnp.concatenate([zero, sin], axis=1), (1, reps))
    return cos_t, s_up, s_dn


def _pad_cols(w, n):
    return jnp.pad(w, ((0, 0), (0, n - w.shape[1])))


def split_w_in(w):
    a0 = 0
    b0 = a0 + A_Q_RANK + A_KV_RANK + A_ROPE
    bw = B_HEADS * 2 * B_QK_DIM
    c0 = b0 + 2 * bw + B_HEADS * B_V_DIM
    cw = C_GROUPS * C_HEADS * HEAD_DIM
    d0 = c0 + 3 * cw
    dq, dk = D_HEADS * HEAD_DIM, D_KV_HEADS * HEAD_DIM
    a_blk = _pad_cols(w[:, a0:b0], N_BV)
    w_n = jnp.concatenate([a_blk, w[:, b0 + 2 * bw:c0], w[:, c0 + 2 * cw:d0],
                           w[:, d0 + dq + dk:d0 + dq + 2 * dk]], axis=1)
    w_r64 = w[:, b0:b0 + 2 * bw]
    w_r128 = jnp.concatenate([w[:, c0:c0 + 2 * cw], w[:, d0:d0 + dq + dk]], axis=1)
    w_n = _pad_cols(w_n, _round_up(N_COLS, 512))
    w_r128 = _pad_cols(w_r128, _round_up(R128_COLS, 512))
    return w_n.astype(BF16), w_r64.astype(BF16), w_r128.astype(BF16)


def split_mla_weights(w_uq, w_ukv):
    d = A_NOPE + A_ROPE
    wq = w_uq.reshape(A_Q_RANK, A_HEADS, d)
    wq = jnp.pad(wq, ((0, 0), (0, 0), (0, A_QK_PAD - d))).reshape(A_Q_RANK, A_HEADS * A_QK_PAD)
    wkv = w_ukv.reshape(A_KV_RANK, A_HEADS, A_NOPE + A_V)
    wkv = jnp.concatenate([wkv[:, :, :A_NOPE].reshape(A_KV_RANK, -1),
                           wkv[:, :, A_NOPE:].reshape(A_KV_RANK, -1)], axis=1)
    return wq.astype(BF16), wkv.astype(BF16)


def _residue_major(a, nseq, seq, dil):
    if dil == 1:
        return a
    w = a.shape[1]
    return a.reshape(nseq, seq // dil, dil, w).transpose(0, 2, 1, 3).reshape(nseq * seq, w)


def _natural_order(a, nseq, seq, dil):
    if dil == 1:
        return a
    w = a.shape[1]
    return a.reshape(nseq, dil, seq // dil, w).transpose(0, 2, 1, 3).reshape(nseq * seq, w)


def _trunk(x, nseq, seq, attn_norm, w_in, mla_q_norm, mla_w_uq, mla_kv_norm, mla_w_ukv,
           diff_lq1, diff_lk1, diff_lq2, diff_lk2, diff_subln, win_sink, w_o,
           ffn_norm, router_group_w, router_group_b, router_expert_w, router_expert_b,
           expert_w_gate, expert_w_up, expert_w_down, final_norm):
    t, d = x.shape
    depth = w_in.shape[0]
    tabs64 = rope_tables(seq, A_ROPE)
    tabs128 = rope_tables(seq, HEAD_DIM)
    hscale = HEAD_DIM ** -0.5
    cs64 = jnp.concatenate([jnp.full((R64_BK,), B_QK_DIM ** -0.5, F32),
                            jnp.ones((R64_BK,), F32)]).reshape(1, -1)
    cs128 = jnp.concatenate([jnp.full((R128_CK,), hscale, F32), jnp.ones((R128_CK,), F32),
                             jnp.full((R128_DK - R128_DQ,), hscale, F32),
                             jnp.ones((_round_up(R128_COLS, 512) - R128_DK,), F32)]).reshape(1, -1)
    cw = C_HEADS * HEAD_DIM
    tm_e = 256

    for i in range(depth):
        lambda_init = 0.8 - 0.6 * math.exp(-0.3 * i)
        w_n, w_r64, w_r128 = split_w_in(w_in[i])
        wq, wkv = split_mla_weights(mla_w_uq[i], mla_w_ukv[i])

        h = rmsnorm_bf16(x, attn_norm[i])
        cols_n = in_projection(h, w_n, seq)
        cols_r64 = in_projection(h, w_r64, seq, rope=(B_QK_DIM // 2, cs64, *tabs64))
        cols_r128 = in_projection(h, w_r128, seq, rope=(HEAD_DIM // 2, cs128, *tabs128))

        qa, ka, va = mla_projection(cols_n, mla_q_norm[i], mla_kv_norm[i], wq, wkv, tabs64, seq)
        o_a = mla_attention(qa, ka, va, nseq, seq)

        lvec = jnp.zeros((8, LANES), F32).at[:4, :B_QK_DIM].set(
            jnp.stack([diff_lq1[i], diff_lk1[i], diff_lq2[i], diff_lk2[i]]))
        o_b = diff_attention(cols_r64, cols_n, lvec, diff_subln[i], lambda_init, nseq, seq)

        outs, stats = [], []
        for g, (window, dil) in enumerate(C_PATTERNS):
            qg = _residue_major(cols_r128[:, R128_CQ + g * cw:R128_CQ + (g + 1) * cw], nseq, seq, dil)
            kg = _residue_major(cols_r128[:, R128_CK + g * cw:R128_CK + (g + 1) * cw], nseq, seq, dil)
            vg = _residue_major(cols_n[:, N_CV + g * cw:N_CV + (g + 1) * cw], nseq, seq, dil)
            og, sg = window_attention(qg, 0, kg, 0, vg, 0, nseq=nseq * dil, length=seq // dil,
                                      heads=C_HEADS, kv_group=1,
                                      half_window=window // (2 * dil), want_stat=True)
            outs.append(_natural_order(og, nseq, seq, dil))
            stats.append(_natural_order(sg, nseq, seq, dil))
        o_c = merge_groups(outs, stats)

        (o_d,) = window_attention(cols_r128, R128_DQ // LANES, cols_r128, R128_DK // LANES,
                                  cols_n, N_DV // LANES, nseq=nseq, length=seq,
                                  heads=D_HEADS, kv_group=D_HEADS // D_KV_HEADS,
                                  half_window=D_WINDOW, sink=win_sink[i])

        x = out_projection([o_a, o_b, o_c, o_d], w_o[i].astype(BF16), x)

        w_r = jnp.concatenate([router_expert_w[i].reshape(d, N_EXPERTS), router_group_w[i]], axis=1)
        w_r = _pad_cols(w_r, LANES)
        w_r_hi = w_r.astype(BF16)
        w_r_lo = (w_r - w_r_hi.astype(F32)).astype(BF16)
        b_r = jnp.concatenate([router_expert_b[i].reshape(-1), router_group_b[i]])
        b_r = jnp.pad(b_r, (0, LANES - b_r.shape[0])).reshape(1, LANES).astype(F32)
        h2, route = norm_and_route(x, ffn_norm[i], w_r_hi, w_r_lo, b_r)

        src_token, gate_rows, tile_expert, tile_valid, pos1, pos2 = dispatch_plan(route, tm_e)
        xg = jnp.take(h2, src_token, axis=0)
        yg = expert_ffn(xg, gate_rows, tile_expert, tile_valid,
                        expert_w_gate[i].astype(BF16), expert_w_up[i].astype(BF16),
                        expert_w_down[i].astype(BF16), tm_e)
        x = combine(x, jnp.take(yg, pos1, axis=0), jnp.take(yg, pos2, axis=0),
                    final_norm, final=(i == depth - 1))
    return x


def kernel(x_prompt, x_sample, attn_norm, w_in, mla_q_norm, mla_w_uq, mla_kv_norm, mla_w_ukv, diff_lq1, diff_lk1, diff_lq2, diff_lk2, diff_subln, win_sink, w_o, ffn_norm, router_group_w, router_group_b, router_expert_w, router_expert_b, expert_w_gate, expert_w_up, expert_w_down, final_norm):
    bp, seq, d = x_prompt.shape
    bs = x_sample.shape[0]
    assert x_sample.shape[1] == seq
    x = jnp.concatenate([x_prompt.reshape(bp * seq, d), x_sample.reshape(bs * seq, d)], axis=0)
    y = _trunk(x, bp + bs, seq, attn_norm, w_in, mla_q_norm, mla_w_uq, mla_kv_norm, mla_w_ukv,
               diff_lq1, diff_lk1, diff_lq2, diff_lk2, diff_subln, win_sink, w_o,
               ffn_norm, router_group_w, router_group_b, router_expert_w, router_expert_b,
               expert_w_gate, expert_w_up, expert_w_down, final_norm)
    return (y[:bp * seq].reshape(bp, seq, d), y[bp * seq:].reshape(bs, seq, d))
```

```python
import functools
import math

import jax
import jax.numpy as jnp
from jax import lax
from jax.experimental import pallas as pl
from jax.experimental.pallas import tpu as pltpu

F32 = jnp.float32
BF16 = jnp.bfloat16

HEAD_DIM = 128
ROPE_THETA = 10000.0
NORM_EPS = 1e-6
NEG_INF = -1e30
LOG2E = math.log2(math.e)

A_HEADS = 8
A_Q_RANK = 1536
A_KV_RANK = 512
A_NOPE = 128
A_ROPE = 64
A_V = 128
A_QK_PAD = 256

B_HEADS = 8
B_QK_DIM = 64
B_V_DIM = 128
B_SUBLN_EPS = 1e-5

C_PATTERNS = ((128, 1), (512, 4), (2048, 16))
C_GROUPS = len(C_PATTERNS)
C_HEADS = 8

D_HEADS = 8
D_KV_HEADS = 2
D_WINDOW = 128

N_GROUPS = 8
EXPERTS_PER_GROUP = 8
N_EXPERTS = N_GROUPS * EXPERTS_PER_GROUP

LANES = 128
VMEM_LIMIT = 56 * 1024 * 1024
HEADS_PER_STEP = 2

N_CQ, N_CKV, N_KPE = 0, A_Q_RANK, A_Q_RANK + A_KV_RANK
N_BV = N_KPE + 2 * LANES
N_CV = N_BV + B_HEADS * B_V_DIM
N_DV = N_CV + C_GROUPS * C_HEADS * HEAD_DIM
N_COLS = N_DV + D_KV_HEADS * HEAD_DIM
R64_BQ, R64_BK = 0, B_HEADS * 2 * B_QK_DIM
R64_COLS = 2 * R64_BK
R128_CQ = 0
R128_CK = C_GROUPS * C_HEADS * HEAD_DIM
R128_DQ = 2 * R128_CK
R128_DK = R128_DQ + D_HEADS * HEAD_DIM
R128_COLS = R128_DK + D_KV_HEADS * HEAD_DIM
PROJ_TN = 512
W_N_COLS = -(-N_COLS // PROJ_TN) * PROJ_TN
W_R128_COLS = -(-R128_COLS // PROJ_TN) * PROJ_TN
W_ALL_COLS = W_N_COLS + R64_COLS + W_R128_COLS


def _tile(n, pref):
    if n <= pref:
        return n
    t = pref
    while n % t:
        t //= 2
    return t


def _params(*sem):
    return pltpu.CompilerParams(dimension_semantics=sem, vmem_limit_bytes=VMEM_LIMIT)


def _dot_nt(a, b):
    return lax.dot_general(a, b, (((1,), (1,)), ((), ())), preferred_element_type=F32)


def _ones_column(rows):
    lane = lax.broadcasted_iota(jnp.int32, (rows, LANES), 1)
    return jnp.where(lane == 0, 1.0, 0.0).astype(BF16)


def _rope_tile(x, cos, s_up, s_dn, half):
    if 2 * half == LANES:
        return x * cos + pltpu.roll(x, half, 1) * s_up
    return (x * cos + pltpu.roll(x, LANES - half, 1) * s_up
            + pltpu.roll(x, half, 1) * s_dn)


def _rmsnorm_kernel(x_ref, g_ref, o_ref):
    x = x_ref[...]
    ms = jnp.mean(x * x, axis=-1, keepdims=True)
    o_ref[...] = (x * lax.rsqrt(ms + NORM_EPS) * g_ref[...]).astype(o_ref.dtype)


def rmsnorm_bf16(x, g):
    t, d = x.shape
    tm = _tile(t, 512)
    return pl.pallas_call(
        _rmsnorm_kernel,
        grid=(t // tm,),
        in_specs=[pl.BlockSpec((tm, d), lambda i: (i, 0)),
                  pl.BlockSpec((1, d), lambda i: (0, 0))],
        out_specs=pl.BlockSpec((tm, d), lambda i: (i, 0)),
        out_shape=jax.ShapeDtypeStruct((t, d), BF16),
        compiler_params=_params("parallel"),
    )(x, g.reshape(1, d))


def _stack_norm_kernel(xa_ref, xb_ref, g_ref, h_ref, x_ref, *, a_tiles):
    def emit(ref):
        x = ref[...]
        ms = jnp.mean(x * x, axis=-1, keepdims=True)
        h_ref[...] = (x * lax.rsqrt(ms + NORM_EPS) * g_ref[...]).astype(h_ref.dtype)
        x_ref[...] = x

    @pl.when(pl.program_id(0) < a_tiles)
    def _():
        emit(xa_ref)

    @pl.when(pl.program_id(0) >= a_tiles)
    def _():
        emit(xb_ref)


def stack_and_norm(xa, xb, g):
    d = xa.shape[1]
    tm = _tile(math.gcd(xa.shape[0], xb.shape[0]), 256)
    a_tiles, b_tiles = xa.shape[0] // tm, xb.shape[0] // tm
    t = xa.shape[0] + xb.shape[0]
    return pl.pallas_call(
        functools.partial(_stack_norm_kernel, a_tiles=a_tiles),
        grid=(a_tiles + b_tiles,),
        in_specs=[pl.BlockSpec((tm, d), lambda i: (jnp.minimum(i, a_tiles - 1), 0)),
                  pl.BlockSpec((tm, d), lambda i: (jnp.maximum(i - a_tiles, 0), 0)),
                  pl.BlockSpec((1, d), lambda i: (0, 0))],
        out_specs=[pl.BlockSpec((tm, d), lambda i: (i, 0)),
                   pl.BlockSpec((tm, d), lambda i: (i, 0))],
        out_shape=[jax.ShapeDtypeStruct((t, d), BF16), jax.ShapeDtypeStruct((t, d), F32)],
        compiler_params=_params("arbitrary"),
    )(xa, xb, g.reshape(1, d))


W_ZERO, W_COPY, W_SHIFT, W_LOW = 0, 1, 2, 3


def _w_in_plan():
    half = LANES // 2
    b0 = A_Q_RANK + A_KV_RANK + A_ROPE
    bw = B_HEADS * 2 * B_QK_DIM
    c0 = b0 + 2 * bw + B_HEADS * B_V_DIM
    cw = C_GROUPS * C_HEADS * HEAD_DIM
    d0 = c0 + 3 * cw
    dq, dk = D_HEADS * HEAD_DIM, D_KV_HEADS * HEAD_DIM
    plan = []

    def shifted(src, width):
        assert src % LANES == half and width % LANES == 0
        for t in range(width // LANES):
            a = (src - half) // LANES + t
            plan.append((W_SHIFT, a, a + 1))

    def pad_to(n):
        while len(plan) * LANES < n:
            plan.append((W_ZERO, 0, 0))

    assert b0 % LANES == half
    for t in range(b0 // LANES):
        plan.append((W_COPY, t, t))
    plan.append((W_LOW, b0 // LANES, b0 // LANES))
    pad_to(N_BV)
    shifted(b0 + 2 * bw, B_HEADS * B_V_DIM)
    shifted(c0 + 2 * cw, cw)
    shifted(d0 + dq + dk, dk)
    pad_to(W_N_COLS)
    shifted(b0, 2 * bw)
    pad_to(W_N_COLS + R64_COLS)
    shifted(c0, 2 * cw)
    shifted(d0, dq + dk)
    pad_to(W_ALL_COLS)
    return tuple(zip(*plan))


def _w_in_prep_kernel(mode_ref, sa_ref, sb_ref, a_ref, b_ref, o_ref):
    mode = mode_ref[pl.program_id(0)]
    low = lax.broadcasted_iota(jnp.int32, (1, LANES), 1) < LANES // 2

    @pl.when(mode == W_ZERO)
    def _():
        o_ref[...] = jnp.zeros_like(o_ref)

    @pl.when(mode == W_COPY)
    def _():
        o_ref[...] = a_ref[...].astype(o_ref.dtype)

    @pl.when(mode == W_LOW)
    def _():
        o_ref[...] = jnp.where(low, a_ref[...], 0.0).astype(o_ref.dtype)

    @pl.when(mode == W_SHIFT)
    def _():
        o_ref[...] = jnp.where(low, pltpu.roll(a_ref[...], LANES // 2, 1),
                               pltpu.roll(b_ref[...], LANES // 2, 1)).astype(o_ref.dtype)


def prepare_w_in(w, layer):
    d = w.shape[1]
    mode, sa, sb = (jnp.asarray(v, jnp.int32) for v in _w_in_plan())
    grid_spec = pltpu.PrefetchScalarGridSpec(
        num_scalar_prefetch=3,
        grid=(W_ALL_COLS // LANES,),
        in_specs=[pl.BlockSpec((None, d, LANES), lambda j, m, a, b: (layer, 0, a[j])),
                  pl.BlockSpec((None, d, LANES), lambda j, m, a, b: (layer, 0, b[j]))],
        out_specs=pl.BlockSpec((d, LANES), lambda j, m, a, b: (0, j)),
    )
    return pl.pallas_call(
        _w_in_prep_kernel,
        grid_spec=grid_spec,
        out_shape=jax.ShapeDtypeStruct((d, W_ALL_COLS), BF16),
        compiler_params=_params("arbitrary"),
    )(mode, sa, sb, w, w)


def _cast_kernel(x_ref, o_ref):
    o_ref[...] = x_ref[...].astype(o_ref.dtype)


def cast_experts_bf16(w, layer):
    _, e, k, n = w.shape
    return pl.pallas_call(
        _cast_kernel,
        grid=(e,),
        in_specs=[pl.BlockSpec((None, None, k, n), lambda i: (layer, i, 0, 0))],
        out_specs=pl.BlockSpec((None, k, n), lambda i: (i, 0, 0)),
        out_shape=jax.ShapeDtypeStruct((e, k, n), BF16),
        compiler_params=_params("parallel"),
    )(w)


def _proj_kernel(*refs, half):
    if half is None:
        h_ref, w_ref, o_ref = refs
        o_ref[...] = jnp.dot(h_ref[...], w_ref[...],
                             preferred_element_type=F32).astype(o_ref.dtype)
        return
    h_ref, w_ref, cs_ref, cos_ref, sup_ref, sdn_ref, o_ref = refs
    acc = jnp.dot(h_ref[...], w_ref[...], preferred_element_type=F32) * cs_ref[...]
    cos, s_up, s_dn = cos_ref[...], sup_ref[...], sdn_ref[...]
    for c in range(acc.shape[1] // LANES):
        sl = slice(c * LANES, (c + 1) * LANES)
        o_ref[:, sl] = _rope_tile(acc[:, sl], cos, s_up, s_dn, half).astype(o_ref.dtype)


def in_projection(h, w_all, col0, n, seq, rope=None):
    t, d = h.shape
    tm = _tile(seq, 1024)
    tn = PROJ_TN
    j0 = col0 // tn
    pos_blocks = seq // tm
    in_specs = [pl.BlockSpec((tm, d), lambda i, j: (i, 0)),
                pl.BlockSpec((d, tn), lambda i, j: (0, j0 + j))]
    args = [h, w_all]
    half = None
    if rope is not None:
        half, colscale, cos, s_up, s_dn = rope
        in_specs.append(pl.BlockSpec((1, tn), lambda i, j: (0, j)))
        tab = pl.BlockSpec((tm, LANES), lambda i, j: (i % pos_blocks, 0))
        in_specs += [tab, tab, tab]
        args += [colscale, cos, s_up, s_dn]
    return pl.pallas_call(
        functools.partial(_proj_kernel, half=half),
        grid=(t // tm, n // tn),
        in_specs=in_specs,
        out_specs=pl.BlockSpec((tm, tn), lambda i, j: (i, j)),
        out_shape=jax.ShapeDtypeStruct((t, n), BF16),
        compiler_params=_params("parallel", "parallel"),
    )(*args)


def _mla_proj_kernel(cq_ref, ckv_ref, kpe_ref, qn_ref, kvn_ref, wuq_ref, wukv_ref,
                     cos_ref, sup_ref, sdn_ref, q_ref, k_ref, v_ref, *, scale):
    cos, s_up, s_dn = cos_ref[...], sup_ref[...], sdn_ref[...]

    def normed(ref, g_ref):
        x = ref[...].astype(F32)
        ms = jnp.mean(x * x, axis=-1, keepdims=True)
        return (x * lax.rsqrt(ms + NORM_EPS) * g_ref[...]).astype(BF16)

    q = jnp.dot(normed(cq_ref, qn_ref), wuq_ref[...], preferred_element_type=F32) * scale
    kv = jnp.dot(normed(ckv_ref, kvn_ref), wukv_ref[...], preferred_element_type=F32)
    kpe = _rope_tile(kpe_ref[...].astype(F32), cos, s_up, s_dn, A_ROPE // 2).astype(BF16)
    for h in range(A_HEADS):
        c0 = h * A_QK_PAD
        q_ref[:, c0:c0 + A_NOPE] = q[:, c0:c0 + A_NOPE].astype(BF16)
        q_ref[:, c0 + A_NOPE:c0 + A_QK_PAD] = _rope_tile(
            q[:, c0 + A_NOPE:c0 + A_QK_PAD], cos, s_up, s_dn, A_ROPE // 2).astype(BF16)
        k_ref[:, c0:c0 + A_NOPE] = kv[:, h * A_NOPE:(h + 1) * A_NOPE].astype(BF16)
        k_ref[:, c0 + A_NOPE:c0 + A_QK_PAD] = kpe
    v_ref[...] = kv[:, A_HEADS * A_NOPE:].astype(BF16)


def mla_projection(cols_n, q_norm, kv_norm, w_uq, w_ukv, tabs64, seq):
    t = cols_n.shape[0]
    tm = _tile(seq, 512)
    pos_blocks = seq // tm
    qw = A_HEADS * A_QK_PAD
    vw = A_HEADS * A_V
    tab = pl.BlockSpec((tm, LANES), lambda i: (i % pos_blocks, 0))
    full = lambda shape: pl.BlockSpec(shape, lambda i: (0, 0))
    return pl.pallas_call(
        functools.partial(_mla_proj_kernel, scale=(A_NOPE + A_ROPE) ** -0.5 * LOG2E),
        grid=(t // tm,),
        in_specs=[pl.BlockSpec((tm, A_Q_RANK), lambda i: (i, N_CQ // A_Q_RANK)),
                  pl.BlockSpec((tm, A_KV_RANK), lambda i: (i, N_CKV // A_KV_RANK)),
                  pl.BlockSpec((tm, LANES), lambda i: (i, N_KPE // LANES)),
                  full((1, A_Q_RANK)), full((1, A_KV_RANK)),
                  full((A_Q_RANK, qw)), full((A_KV_RANK, qw)),
                  tab, tab, tab],
        out_specs=[pl.BlockSpec((tm, qw), lambda i: (i, 0)),
                   pl.BlockSpec((tm, qw), lambda i: (i, 0)),
                   pl.BlockSpec((tm, vw), lambda i: (i, 0))],
        out_shape=[jax.ShapeDtypeStruct((t, qw), BF16),
                   jax.ShapeDtypeStruct((t, qw), BF16),
                   jax.ShapeDtypeStruct((t, vw), BF16)],
        compiler_params=_params("parallel"),
    )(cols_n, cols_n, cols_n, q_norm.reshape(1, -1), kv_norm.reshape(1, -1),
      w_uq, w_ukv, *tabs64)


def _softmax_pv(s, v):
    m = jnp.max(s, axis=-1, keepdims=True)
    e = jnp.exp2(s - m)
    l = jnp.sum(e, axis=-1, keepdims=True)
    return jnp.dot(e.astype(BF16), v, preferred_element_type=F32) / l


def _full_attn_kernel(q_ref, k_ref, v_ref, o_ref):
    heads = o_ref.shape[1] // A_V
    s = [_dot_nt(q_ref[:, h * A_QK_PAD:(h + 1) * A_QK_PAD], k_ref[:, h * A_QK_PAD:(h + 1) * A_QK_PAD])
         for h in range(heads)]
    for h in range(heads):
        sl = slice(h * A_V, (h + 1) * A_V)
        o_ref[:, sl] = _softmax_pv(s[h], v_ref[:, sl]).astype(o_ref.dtype)


def mla_attention(q, k, v, nseq, seq):
    tq = _tile(seq, 256)
    nq = seq // tq
    hp = HEADS_PER_STEP
    return pl.pallas_call(
        _full_attn_kernel,
        grid=(nseq, A_HEADS // hp, nq),
        in_specs=[pl.BlockSpec((tq, hp * A_QK_PAD), lambda b, h, i: (b * nq + i, h)),
                  pl.BlockSpec((seq, hp * A_QK_PAD), lambda b, h, i: (b, h)),
                  pl.BlockSpec((seq, hp * A_V), lambda b, h, i: (b, h))],
        out_specs=pl.BlockSpec((tq, hp * A_V), lambda b, h, i: (b * nq + i, h)),
        out_shape=jax.ShapeDtypeStruct((nseq * seq, A_HEADS * A_V), BF16),
        compiler_params=_params("parallel", "parallel", "parallel"),
    )(q, k, v)


def _diff_attn_kernel(q_ref, k_ref, v_ref, lv_ref, g_ref, o_ref, *, lambda_init):
    lv = lv_ref[...]
    lam = (jnp.exp(jnp.sum(lv[0:1] * lv[1:2], axis=-1, keepdims=True))
           - jnp.exp(jnp.sum(lv[2:3] * lv[3:4], axis=-1, keepdims=True)) + lambda_init)
    lane = lax.broadcasted_iota(jnp.int32, (1, LANES), 1)
    heads = o_ref.shape[1] // B_V_DIM
    s = []
    for h in range(heads):
        sl = slice(h * LANES, (h + 1) * LANES)
        q, k = q_ref[:, sl], k_ref[:, sl]
        zero = jnp.zeros_like(q)
        s.append(_dot_nt(jnp.where(lane < B_QK_DIM, q, zero), k))
        s.append(_dot_nt(jnp.where(lane >= B_QK_DIM, q, zero), k))
    for h in range(heads):
        sl = slice(h * B_V_DIM, (h + 1) * B_V_DIM)
        v = v_ref[:, sl]
        o = _softmax_pv(s[2 * h], v) - lam * _softmax_pv(s[2 * h + 1], v)
        ms = jnp.mean(o * o, axis=-1, keepdims=True)
        o = o * lax.rsqrt(ms + B_SUBLN_EPS) * g_ref[...] * (1.0 - lambda_init)
        o_ref[:, sl] = o.astype(o_ref.dtype)


def diff_attention(cols_r64, cols_n, lvec, subln_g, lambda_init, nseq, seq):
    tq = _tile(seq, 256)
    nq = seq // tq
    hp = HEADS_PER_STEP
    w = hp * LANES
    qb, kb, vb = R64_BQ // w, R64_BK // w, N_BV // w
    return pl.pallas_call(
        functools.partial(_diff_attn_kernel, lambda_init=lambda_init),
        grid=(nseq, B_HEADS // hp, nq),
        in_specs=[pl.BlockSpec((tq, w), lambda b, h, i: (b * nq + i, qb + h)),
                  pl.BlockSpec((seq, w), lambda b, h, i: (b, kb + h)),
                  pl.BlockSpec((seq, w), lambda b, h, i: (b, vb + h)),
                  pl.BlockSpec((8, LANES), lambda b, h, i: (0, 0)),
                  pl.BlockSpec((1, B_V_DIM), lambda b, h, i: (0, 0))],
        out_specs=pl.BlockSpec((tq, w), lambda b, h, i: (b * nq + i, h)),
        out_shape=jax.ShapeDtypeStruct((nseq * seq, B_HEADS * B_V_DIM), BF16),
        compiler_params=_params("parallel", "parallel", "parallel"),
    )(cols_r64, cols_r64, cols_n, lvec, subln_g.reshape(1, -1))


def _window_attn_kernel(*refs, length, half_window, has_sink, want_stat):
    refs = list(refs)
    sink_ref = refs.pop(0) if has_sink else None
    q_ref, k_ref, v_ref, o_ref = refs[:4]
    stat_ref = refs[4] if want_stat else None
    h = pl.program_id(1)
    qb = min(LANES, length)
    kw = min(qb + 2 * half_window, length)
    lane = lax.broadcasted_iota(jnp.int32, (1, LANES), 1)
    ones = _ones_column(kw)

    if want_stat:
        @pl.when(h == 0)
        def _():
            stat_ref[...] = jnp.zeros_like(stat_ref)

    def body(i, carry):
        r0 = pl.multiple_of(i * qb, qb)
        ks = pl.multiple_of(jnp.clip(r0 - half_window, 0, length - kw), 64)
        s = _dot_nt(q_ref[pl.ds(r0, qb), :], k_ref[pl.ds(ks, kw), :])
        qpos = r0 + lax.broadcasted_iota(jnp.int32, (qb, 1), 0)
        kpos = ks + lax.broadcasted_iota(jnp.int32, (1, kw), 1)
        s = jnp.where(jnp.abs(kpos - qpos) <= half_window, s, NEG_INF)
        m = jnp.max(s, axis=-1, keepdims=True)
        if has_sink:
            m = jnp.maximum(m, sink_ref[h])
        v = jnp.concatenate([v_ref[pl.ds(ks, kw), :], ones], axis=-1)
        o = jnp.dot(jnp.exp2(s - m).astype(BF16), v, preferred_element_type=F32)
        l = o[:, HEAD_DIM:HEAD_DIM + 1]
        if has_sink:
            l = l + jnp.exp2(sink_ref[h] - m)
        o_ref[pl.ds(r0, qb), :] = (o[:, :HEAD_DIM] / l).astype(o_ref.dtype)
        if want_stat:
            lse = m + jnp.log2(l)
            stat_ref[pl.ds(r0, qb), :] = jnp.where(lane == h, lse, stat_ref[pl.ds(r0, qb), :])
        return carry

    n_blocks = length // qb
    lax.fori_loop(0, n_blocks, body, 0, unroll=min(4, n_blocks))


def window_attention(q_arr, q_blk, k_arr, k_blk, v_arr, v_blk, *, nseq, length,
                     heads, kv_group, half_window, sink=None, want_stat=False):
    spec = lambda blk, grp: pl.BlockSpec((length, HEAD_DIM),
                                         lambda b, h: (b, blk + h // grp))
    in_specs = [spec(q_blk, 1), spec(k_blk, kv_group), spec(v_blk, kv_group)]
    args = [q_arr, k_arr, v_arr]
    if sink is not None:
        in_specs.insert(0, pl.BlockSpec(memory_space=pltpu.SMEM))
        args.insert(0, sink.astype(F32))
    out_specs = [pl.BlockSpec((length, HEAD_DIM), lambda b, h: (b, h))]
    out_shape = [jax.ShapeDtypeStruct((nseq * length, heads * HEAD_DIM), BF16)]
    if want_stat:
        out_specs.append(pl.BlockSpec((length, LANES), lambda b, h: (b, 0)))
        out_shape.append(jax.ShapeDtypeStruct((nseq * length, LANES), F32))
    return pl.pallas_call(
        functools.partial(_window_attn_kernel, length=length, half_window=half_window,
                          has_sink=sink is not None, want_stat=want_stat),
        grid=(nseq, heads),
        in_specs=in_specs,
        out_specs=out_specs,
        out_shape=out_shape,
        compiler_params=_params("parallel", "arbitrary"),
    )(*args)


def _merge_kernel(*refs):
    o_refs, s_refs, out_ref = refs[:C_GROUPS], refs[C_GROUPS:2 * C_GROUPS], refs[-1]
    lse = [r[...] for r in s_refs]
    m = functools.reduce(jnp.maximum, lse)
    w = [jnp.exp2(x - m) for x in lse]
    inv = 1.0 / functools.reduce(lambda a, b: a + b, w)
    w = [x * inv for x in w]
    for h in range(C_HEADS):
        sl = slice(h * HEAD_DIM, (h + 1) * HEAD_DIM)
        acc = w[0][:, h:h + 1] * o_refs[0][:, sl].astype(F32)
        for g in range(1, C_GROUPS):
            acc = acc + w[g][:, h:h + 1] * o_refs[g][:, sl].astype(F32)
        out_ref[:, sl] = acc.astype(out_ref.dtype)


def merge_groups(outs, stats):
    t, w = outs[0].shape
    tm = _tile(t, 512)
    return pl.pallas_call(
        _merge_kernel,
        grid=(t // tm,),
        in_specs=([pl.BlockSpec((tm, w), lambda i: (i, 0))] * C_GROUPS
                  + [pl.BlockSpec((tm, LANES), lambda i: (i, 0))] * C_GROUPS),
        out_specs=pl.BlockSpec((tm, w), lambda i: (i, 0)),
        out_shape=jax.ShapeDtypeStruct((t, w), BF16),
        compiler_params=_params("parallel"),
    )(*outs, *stats)


def _out_proj_kernel(a_ref, b_ref, c_ref, d_ref, w_ref, x_ref, o_ref):
    o = jnp.concatenate([a_ref[...], b_ref[...], c_ref[...], d_ref[...]], axis=-1)
    o_ref[...] = x_ref[...] + jnp.dot(o, w_ref[...], preferred_element_type=F32)


def out_projection(parts, w_o, x):
    t, d = x.shape
    tm = _tile(t, 1024)
    tn = _tile(d, 512)
    in_specs = [pl.BlockSpec((tm, p.shape[1]), lambda i, j: (i, 0)) for p in parts]
    in_specs += [pl.BlockSpec((w_o.shape[0], tn), lambda i, j: (0, j)),
                 pl.BlockSpec((tm, tn), lambda i, j: (i, j))]
    return pl.pallas_call(
        _out_proj_kernel,
        grid=(t // tm, d // tn),
        in_specs=in_specs,
        out_specs=pl.BlockSpec((tm, tn), lambda i, j: (i, j)),
        out_shape=jax.ShapeDtypeStruct((t, d), F32),
        compiler_params=_params("parallel", "parallel"),
    )(*parts, w_o, x)


def _router_kernel(x_ref, g_ref, whi_ref, wlo_ref, b_ref, h_ref, r_ref):
    x = x_ref[...]
    ms = jnp.mean(x * x, axis=-1, keepdims=True)
    hn = x * lax.rsqrt(ms + NORM_EPS) * g_ref[...]
    hi = hn.astype(BF16)
    h_ref[...] = hi
    lo = (hn - hi.astype(F32)).astype(BF16)
    w_hi = whi_ref[...]
    logits = (jnp.dot(hi, w_hi, preferred_element_type=F32)
              + jnp.dot(lo, w_hi, preferred_element_type=F32)
              + jnp.dot(hi, wlo_ref[...], preferred_element_type=F32)) + b_ref[...]
    lane = lax.broadcasted_iota(jnp.int32, logits.shape, 1)
    ninf = float("-inf")
    big = jnp.int32(LANES)
    gmask = (lane >= N_EXPERTS) & (lane < N_EXPERTS + N_GROUPS)
    gl = jnp.where(gmask, logits, ninf)
    gmax = jnp.max(gl, axis=-1, keepdims=True)
    g_idx = jnp.min(jnp.where(gl == gmax, lane - N_EXPERTS, big), axis=-1, keepdims=True)
    g_w = 1.0 / jnp.sum(jnp.exp(gl - gmax), axis=-1, keepdims=True)
    emask = (lane < N_EXPERTS) & ((lane // EXPERTS_PER_GROUP) == g_idx)
    el = jnp.where(emask, logits, ninf)
    v1 = jnp.max(el, axis=-1, keepdims=True)
    i1 = jnp.min(jnp.where(el == v1, lane, big), axis=-1, keepdims=True)
    el2 = jnp.where(lane == i1, ninf, el)
    v2 = jnp.max(el2, axis=-1, keepdims=True)
    i2 = jnp.min(jnp.where(el2 == v2, lane, big), axis=-1, keepdims=True)
    t = jnp.exp(v2 - v1)
    w1 = g_w / (1.0 + t)
    w2 = w1 * t
    r_ref[...] = jnp.where(lane == 0, i1.astype(F32),
                           jnp.where(lane == 1, i2.astype(F32),
                                     jnp.where(lane == 2, w1, jnp.where(lane == 3, w2, 0.0))))


def norm_and_route(x, g, w_hi, w_lo, bias):
    t, d = x.shape
    tm = _tile(t, 512)
    full = lambda shape: pl.BlockSpec(shape, lambda i: (0, 0))
    return pl.pallas_call(
        _router_kernel,
        grid=(t // tm,),
        in_specs=[pl.BlockSpec((tm, d), lambda i: (i, 0)), full((1, d)),
                  full((d, LANES)), full((d, LANES)), full((1, LANES))],
        out_specs=[pl.BlockSpec((tm, d), lambda i: (i, 0)),
                   pl.BlockSpec((tm, LANES), lambda i: (i, 0))],
        out_shape=[jax.ShapeDtypeStruct((t, d), BF16),
                   jax.ShapeDtypeStruct((t, LANES), F32)],
        compiler_params=_params("parallel"),
    )(x, g.reshape(1, d), w_hi, w_lo, bias)


def _expert_kernel(te_ref, tv_ref, x_ref, gate_ref, wg_ref, wu_ref, wd_ref, o_ref):
    j = pl.program_id(0)

    @pl.when(tv_ref[j] != 0)
    def _():
        x = x_ref[...]
        hg = jnp.dot(x, wg_ref[...], preferred_element_type=F32)
        hu = jnp.dot(x, wu_ref[...], preferred_element_type=F32)
        a = hg / (1.0 + jnp.exp(-hg)) * hu * gate_ref[...]
        o_ref[...] = jnp.dot(a.astype(BF16), wd_ref[...], preferred_element_type=F32)

    @pl.when(tv_ref[j] == 0)
    def _():
        o_ref[...] = jnp.zeros_like(o_ref)


def expert_ffn(xg, gate_rows, tile_expert, tile_valid, w_gate, w_up, w_down, tm):
    r, d = xg.shape
    ff = w_gate.shape[2]
    grid_spec = pltpu.PrefetchScalarGridSpec(
        num_scalar_prefetch=2,
        grid=(r // tm,),
        in_specs=[pl.BlockSpec((tm, d), lambda j, te, tv: (j, 0)),
                  pl.BlockSpec((tm, 1), lambda j, te, tv: (j, 0)),
                  pl.BlockSpec((None, d, ff), lambda j, te, tv: (te[j], 0, 0)),
                  pl.BlockSpec((None, d, ff), lambda j, te, tv: (te[j], 0, 0)),
                  pl.BlockSpec((None, ff, d), lambda j, te, tv: (te[j], 0, 0))],
        out_specs=pl.BlockSpec((tm, d), lambda j, te, tv: (j, 0)),
    )
    return pl.pallas_call(
        _expert_kernel,
        grid_spec=grid_spec,
        out_shape=jax.ShapeDtypeStruct((r, d), F32),
        compiler_params=_params("arbitrary"),
    )(tile_expert, tile_valid, xg, gate_rows, w_gate, w_up, w_down)


def _combine_kernel(x_ref, y1_ref, y2_ref, g_ref, o_ref, *, final):
    x = x_ref[...] + y1_ref[...] + y2_ref[...]
    if final:
        ms = jnp.mean(x * x, axis=-1, keepdims=True)
        x = x * lax.rsqrt(ms + NORM_EPS) * g_ref[...]
    o_ref[...] = x


def combine(x, y1, y2, g, final, row0=0, nrows=None):
    d = x.shape[1]
    nrows = x.shape[0] if nrows is None else nrows
    tm = _tile(math.gcd(nrows, row0) if row0 else nrows, 256)
    i0 = row0 // tm
    row = pl.BlockSpec((tm, d), lambda i: (i0 + i, 0))
    return pl.pallas_call(
        functools.partial(_combine_kernel, final=final),
        grid=(nrows // tm,),
        in_specs=[row, row, row, pl.BlockSpec((1, d), lambda i: (0, 0))],
        out_specs=pl.BlockSpec((tm, d), lambda i: (i, 0)),
        out_shape=jax.ShapeDtypeStruct((nrows, d), F32),
        compiler_params=_params("parallel"),
    )(x, y1, y2, g.reshape(1, d))


def dispatch_plan(route, tm):
    t = route.shape[0]
    e_flat = jnp.concatenate([route[:, 0], route[:, 1]]).astype(jnp.int32)
    g_flat = jnp.concatenate([route[:, 2], route[:, 3]])
    pair = jnp.arange(2 * t, dtype=jnp.int32)
    e_sorted, order = lax.sort((e_flat, pair), num_keys=1, is_stable=True)
    bounds = jnp.searchsorted(e_sorted, jnp.arange(N_EXPERTS + 1, dtype=jnp.int32),
                              side="left").astype(jnp.int32)
    start, counts = bounds[:-1], bounds[1:] - bounds[:-1]
    padded = (counts + tm - 1) // tm * tm
    pad_end = jnp.cumsum(padded)
    pad_start = pad_end - padded
    rows = 2 * t + N_EXPERTS * tm
    tile_start = jnp.arange(rows // tm, dtype=jnp.int32) * tm
    tile_valid = tile_start < pad_end[-1]
    tile_expert = jnp.minimum(jnp.searchsorted(pad_end, tile_start, side="right"),
                              N_EXPERTS - 1).astype(jnp.int32)
    e_row = jnp.repeat(tile_expert, tm)
    k = jnp.arange(rows, dtype=jnp.int32) - pad_start[e_row]
    valid = (k < counts[e_row]) & jnp.repeat(tile_valid, tm)
    src_pair = order[jnp.clip(start[e_row] + k, 0, 2 * t - 1)]
    src_token = jnp.where(valid, src_pair % t, 0)
    gate_rows = jnp.where(valid, g_flat[src_pair], 0.0)
    dest_sorted = pad_start[e_sorted] + pair - start[e_sorted]
    _, pos = lax.sort((order, dest_sorted), num_keys=1)
    return (src_token, gate_rows.reshape(rows, 1), tile_expert,
            tile_valid.astype(jnp.int32), pos[:t], pos[t:])


def rope_tables(seq, dim):
    inv = 1.0 / (ROPE_THETA ** (jnp.arange(0, dim, 2, dtype=F32) / dim))
    ang = jnp.arange(seq, dtype=F32)[:, None] * inv[None, :]
    cos, sin = jnp.cos(ang), jnp.sin(ang)
    reps = LANES // dim
    zero = jnp.zeros_like(sin)
    cos_t = jnp.tile(jnp.concatenate([cos, cos], axis=1), (1, reps))
    if reps == 1:
        return cos_t, jnp.concatenate([-sin, sin], axis=1), zero
    s_up = jnp.tile(jnp.concatenate([-sin, zero], axis=1), (1, reps))
    s_dn = jnp.tile(jnp.concatenate([zero, sin], axis=1), (1, reps))
    return cos_t, s_up, s_dn


def _pad_cols(w, n):
    return jnp.pad(w, ((0, 0), (0, n - w.shape[1])))


def split_mla_weights(w_uq, w_ukv):
    d = A_NOPE + A_ROPE
    wq = w_uq.reshape(A_Q_RANK, A_HEADS, d)
    wq = jnp.pad(wq, ((0, 0), (0, 0), (0, A_QK_PAD - d))).reshape(A_Q_RANK, A_HEADS * A_QK_PAD)
    wkv = w_ukv.reshape(A_KV_RANK, A_HEADS, A_NOPE + A_V)
    wkv = jnp.concatenate([wkv[:, :, :A_NOPE].reshape(A_KV_RANK, -1),
                           wkv[:, :, A_NOPE:].reshape(A_KV_RANK, -1)], axis=1)
    return wq.astype(BF16), wkv.astype(BF16)


def _residue_major(a, nseq, seq, dil):
    if dil == 1:
        return a
    w = a.shape[1]
    return a.reshape(nseq, seq // dil, dil, w).transpose(0, 2, 1, 3).reshape(nseq * seq, w)


def _natural_order(a, nseq, seq, dil):
    if dil == 1:
        return a
    w = a.shape[1]
    return a.reshape(nseq, dil, seq // dil, w).transpose(0, 2, 1, 3).reshape(nseq * seq, w)


def kernel(x_prompt, x_sample, attn_norm, w_in, mla_q_norm, mla_w_uq, mla_kv_norm, mla_w_ukv, diff_lq1, diff_lk1, diff_lq2, diff_lk2, diff_subln, win_sink, w_o, ffn_norm, router_group_w, router_group_b, router_expert_w, router_expert_b, expert_w_gate, expert_w_up, expert_w_down, final_norm):
    bp, seq, d = x_prompt.shape
    bs = x_sample.shape[0]
    assert x_sample.shape[1:] == (seq, d)
    nseq = bp + bs
    depth = w_in.shape[0]
    tabs64 = rope_tables(seq, A_ROPE)
    tabs128 = rope_tables(seq, HEAD_DIM)
    hscale = HEAD_DIM ** -0.5 * LOG2E
    cs64 = jnp.concatenate([jnp.full((R64_BK,), B_QK_DIM ** -0.5 * LOG2E, F32),
                            jnp.ones((R64_BK,), F32)]).reshape(1, -1)
    cs128 = jnp.concatenate([jnp.full((R128_CK,), hscale, F32), jnp.ones((R128_CK,), F32),
                             jnp.full((R128_DK - R128_DQ,), hscale, F32),
                             jnp.ones((W_R128_COLS - R128_DK,), F32)]).reshape(1, -1)
    cw = C_HEADS * HEAD_DIM
    tm_e = 256

    x = None
    for i in range(depth):
        lambda_init = 0.8 - 0.6 * math.exp(-0.3 * i)
        w_all = prepare_w_in(w_in, i)
        wq, wkv = split_mla_weights(mla_w_uq[i], mla_w_ukv[i])

        if i == 0:
            h, x = stack_and_norm(x_prompt.reshape(bp * seq, d), x_sample.reshape(bs * seq, d),
                                  attn_norm[i])
        else:
            h = rmsnorm_bf16(x, attn_norm[i])
        cols_n = in_projection(h, w_all, 0, W_N_COLS, seq)
        cols_r64 = in_projection(h, w_all, W_N_COLS, R64_COLS, seq,
                                 rope=(B_QK_DIM // 2, cs64, *tabs64))
        cols_r128 = in_projection(h, w_all, W_N_COLS + R64_COLS, W_R128_COLS, seq,
                                  rope=(HEAD_DIM // 2, cs128, *tabs128))

        qa, ka, va = mla_projection(cols_n, mla_q_norm[i], mla_kv_norm[i], wq, wkv, tabs64, seq)
        o_a = mla_attention(qa, ka, va, nseq, seq)

        lvec = jnp.zeros((8, LANES), F32).at[:4, :B_QK_DIM].set(
            jnp.stack([diff_lq1[i], diff_lk1[i], diff_lq2[i], diff_lk2[i]]))
        o_b = diff_attention(cols_r64, cols_n, lvec, diff_subln[i], lambda_init, nseq, seq)

        outs, stats = [], []
        for g, (window, dil) in enumerate(C_PATTERNS):
            qg = _residue_major(cols_r128[:, R128_CQ + g * cw:R128_CQ + (g + 1) * cw], nseq, seq, dil)
            kg = _residue_major(cols_r128[:, R128_CK + g * cw:R128_CK + (g + 1) * cw], nseq, seq, dil)
            vg = _residue_major(cols_n[:, N_CV + g * cw:N_CV + (g + 1) * cw], nseq, seq, dil)
            og, sg = window_attention(qg, 0, kg, 0, vg, 0, nseq=nseq * dil, length=seq // dil,
                                      heads=C_HEADS, kv_group=1,
                                      half_window=window // (2 * dil), want_stat=True)
            outs.append(_natural_order(og, nseq, seq, dil))
            stats.append(_natural_order(sg, nseq, seq, dil))
        o_c = merge_groups(outs, stats)

        (o_d,) = window_attention(cols_r128, R128_DQ // LANES, cols_r128, R128_DK // LANES,
                                  cols_n, N_DV // LANES, nseq=nseq, length=seq,
                                  heads=D_HEADS, kv_group=D_HEADS // D_KV_HEADS,
                                  half_window=D_WINDOW, sink=win_sink[i] * LOG2E)

        x = out_projection([o_a, o_b, o_c, o_d], w_o[i].astype(BF16), x)

        w_r = jnp.concatenate([router_expert_w[i].reshape(d, N_EXPERTS), router_group_w[i]], axis=1)
        w_r = _pad_cols(w_r, LANES)
        w_r_hi = w_r.astype(BF16)
        w_r_lo = (w_r - w_r_hi.astype(F32)).astype(BF16)
        b_r = jnp.concatenate([router_expert_b[i].reshape(-1), router_group_b[i]])
        b_r = jnp.pad(b_r, (0, LANES - b_r.shape[0])).reshape(1, LANES).astype(F32)
        h2, route = norm_and_route(x, ffn_norm[i], w_r_hi, w_r_lo, b_r)

        src_token, gate_rows, tile_expert, tile_valid, pos1, pos2 = dispatch_plan(route, tm_e)
        xg = jnp.take(h2, src_token, axis=0)
        yg = expert_ffn(xg, gate_rows, tile_expert, tile_valid,
                        cast_experts_bf16(expert_w_gate, i), cast_experts_bf16(expert_w_up, i),
                        cast_experts_bf16(expert_w_down, i), tm_e)
        y1, y2 = jnp.take(yg, pos1, axis=0), jnp.take(yg, pos2, axis=0)
        if i < depth - 1:
            x = combine(x, y1, y2, final_norm, final=False)
    y_p = combine(x, y1, y2, final_norm, final=True, row0=0, nrows=bp * seq)
    y_s = combine(x, y1, y2, final_norm, final=True, row0=bp * seq, nrows=bs * seq)
    return (y_p.reshape(bp, seq, d), y_s.reshape(bs, seq, d))
```

```python
import functools
import math

import jax
import jax.numpy as jnp
from jax import lax
from jax.experimental import pallas as pl
from jax.experimental.pallas import tpu as pltpu

F32 = jnp.float32
BF16 = jnp.bfloat16

HEAD_DIM = 128
ROPE_THETA = 10000.0
NORM_EPS = 1e-6
NEG_INF = -1e30
LOG2E = math.log2(math.e)

A_HEADS = 8
A_Q_RANK = 1536
A_KV_RANK = 512
A_NOPE = 128
A_ROPE = 64
A_V = 128
A_QK_PAD = 256

B_HEADS = 8
B_QK_DIM = 64
B_V_DIM = 128
B_SUBLN_EPS = 1e-5

C_PATTERNS = ((128, 1), (512, 4), (2048, 16))
C_GROUPS = len(C_PATTERNS)
C_HEADS = 8

D_HEADS = 8
D_KV_HEADS = 2
D_WINDOW = 128

N_GROUPS = 8
EXPERTS_PER_GROUP = 8
N_EXPERTS = N_GROUPS * EXPERTS_PER_GROUP

LANES = 128
VMEM_LIMIT = 56 * 1024 * 1024
HEADS_PER_STEP = 2
WINDOW_BLOCK_ELEMS = 8192

N_CQ, N_CKV, N_KPE = 0, A_Q_RANK, A_Q_RANK + A_KV_RANK
N_BV = N_KPE + 2 * LANES
N_CV = N_BV + B_HEADS * B_V_DIM
N_DV = N_CV + C_GROUPS * C_HEADS * HEAD_DIM
N_COLS = N_DV + D_KV_HEADS * HEAD_DIM
R64_BQ, R64_BK = 0, B_HEADS * 2 * B_QK_DIM
R64_COLS = 2 * R64_BK
R128_CQ = 0
R128_CK = C_GROUPS * C_HEADS * HEAD_DIM
R128_DQ = 2 * R128_CK
R128_DK = R128_DQ + D_HEADS * HEAD_DIM
R128_COLS = R128_DK + D_KV_HEADS * HEAD_DIM
PROJ_TN = 512
W_N_COLS = -(-N_COLS // PROJ_TN) * PROJ_TN
W_R128_COLS = -(-R128_COLS // PROJ_TN) * PROJ_TN
W_ALL_COLS = W_N_COLS + R64_COLS + W_R128_COLS


def _tile(n, pref):
    if n <= pref:
        return n
    t = pref
    while n % t:
        t //= 2
    return t


def _params(*sem):
    return pltpu.CompilerParams(dimension_semantics=sem, vmem_limit_bytes=VMEM_LIMIT)


def _dot_nt(a, b):
    return lax.dot_general(a, b, (((1,), (1,)), ((), ())), preferred_element_type=F32)


def _ones_column(rows):
    lane = lax.broadcasted_iota(jnp.int32, (rows, LANES), 1)
    return jnp.where(lane == 0, 1.0, 0.0).astype(BF16)


def _rope_tile(x, cos, s_up, s_dn, half):
    if 2 * half == LANES:
        return x * cos + pltpu.roll(x, half, 1) * s_up
    return (x * cos + pltpu.roll(x, LANES - half, 1) * s_up
            + pltpu.roll(x, half, 1) * s_dn)


def _rmsnorm_kernel(x_ref, g_ref, o_ref):
    x = x_ref[...]
    ms = jnp.mean(x * x, axis=-1, keepdims=True)
    o_ref[...] = (x * lax.rsqrt(ms + NORM_EPS) * g_ref[...]).astype(o_ref.dtype)


def rmsnorm_bf16(x, g):
    t, d = x.shape
    tm = _tile(t, 512)
    return pl.pallas_call(
        _rmsnorm_kernel,
        grid=(t // tm,),
        in_specs=[pl.BlockSpec((tm, d), lambda i: (i, 0)),
                  pl.BlockSpec((1, d), lambda i: (0, 0))],
        out_specs=pl.BlockSpec((tm, d), lambda i: (i, 0)),
        out_shape=jax.ShapeDtypeStruct((t, d), BF16),
        compiler_params=_params("parallel"),
    )(x, g.reshape(1, d))


def _stack_norm_kernel(xa_ref, xb_ref, g_ref, h_ref, x_ref, *, a_tiles):
    def emit(ref):
        x = ref[...]
        ms = jnp.mean(x * x, axis=-1, keepdims=True)
        h_ref[...] = (x * lax.rsqrt(ms + NORM_EPS) * g_ref[...]).astype(h_ref.dtype)
        x_ref[...] = x

    @pl.when(pl.program_id(0) < a_tiles)
    def _():
        emit(xa_ref)

    @pl.when(pl.program_id(0) >= a_tiles)
    def _():
        emit(xb_ref)


def stack_and_norm(xa, xb, g):
    d = xa.shape[1]
    tm = _tile(math.gcd(xa.shape[0], xb.shape[0]), 256)
    a_tiles, b_tiles = xa.shape[0] // tm, xb.shape[0] // tm
    t = xa.shape[0] + xb.shape[0]
    return pl.pallas_call(
        functools.partial(_stack_norm_kernel, a_tiles=a_tiles),
        grid=(a_tiles + b_tiles,),
        in_specs=[pl.BlockSpec((tm, d), lambda i: (jnp.minimum(i, a_tiles - 1), 0)),
                  pl.BlockSpec((tm, d), lambda i: (jnp.maximum(i - a_tiles, 0), 0)),
                  pl.BlockSpec((1, d), lambda i: (0, 0))],
        out_specs=[pl.BlockSpec((tm, d), lambda i: (i, 0)),
                   pl.BlockSpec((tm, d), lambda i: (i, 0))],
        out_shape=[jax.ShapeDtypeStruct((t, d), BF16), jax.ShapeDtypeStruct((t, d), F32)],
        compiler_params=_params("arbitrary"),
    )(xa, xb, g.reshape(1, d))


W_ZERO, W_COPY, W_SHIFT, W_LOW = 0, 1, 2, 3


def _w_in_plan():
    half = LANES // 2
    b0 = A_Q_RANK + A_KV_RANK + A_ROPE
    bw = B_HEADS * 2 * B_QK_DIM
    c0 = b0 + 2 * bw + B_HEADS * B_V_DIM
    cw = C_GROUPS * C_HEADS * HEAD_DIM
    d0 = c0 + 3 * cw
    dq, dk = D_HEADS * HEAD_DIM, D_KV_HEADS * HEAD_DIM
    plan = []

    def shifted(src, width):
        assert src % LANES == half and width % LANES == 0
        for t in range(width // LANES):
            a = (src - half) // LANES + t
            plan.append((W_SHIFT, a, a + 1))

    def pad_to(n):
        while len(plan) * LANES < n:
            plan.append((W_ZERO, 0, 0))

    assert b0 % LANES == half
    for t in range(b0 // LANES):
        plan.append((W_COPY, t, t))
    plan.append((W_LOW, b0 // LANES, b0 // LANES))
    pad_to(N_BV)
    shifted(b0 + 2 * bw, B_HEADS * B_V_DIM)
    shifted(c0 + 2 * cw, cw)
    shifted(d0 + dq + dk, dk)
    pad_to(W_N_COLS)
    shifted(b0, 2 * bw)
    pad_to(W_N_COLS + R64_COLS)
    shifted(c0, 2 * cw)
    shifted(d0, dq + dk)
    pad_to(W_ALL_COLS)
    return tuple(zip(*plan))


def _w_in_prep_kernel(mode_ref, sa_ref, sb_ref, a_ref, b_ref, o_ref):
    mode = mode_ref[pl.program_id(0)]
    low = lax.broadcasted_iota(jnp.int32, (1, LANES), 1) < LANES // 2

    @pl.when(mode == W_ZERO)
    def _():
        o_ref[...] = jnp.zeros_like(o_ref)

    @pl.when(mode == W_COPY)
    def _():
        o_ref[...] = a_ref[...].astype(o_ref.dtype)

    @pl.when(mode == W_LOW)
    def _():
        o_ref[...] = jnp.where(low, a_ref[...], 0.0).astype(o_ref.dtype)

    @pl.when(mode == W_SHIFT)
    def _():
        o_ref[...] = jnp.where(low, pltpu.roll(a_ref[...], LANES // 2, 1),
                               pltpu.roll(b_ref[...], LANES // 2, 1)).astype(o_ref.dtype)


def prepare_w_in(w, layer):
    d = w.shape[1]
    mode, sa, sb = (jnp.asarray(v, jnp.int32) for v in _w_in_plan())
    grid_spec = pltpu.PrefetchScalarGridSpec(
        num_scalar_prefetch=3,
        grid=(W_ALL_COLS // LANES,),
        in_specs=[pl.BlockSpec((None, d, LANES), lambda j, m, a, b: (layer, 0, a[j])),
                  pl.BlockSpec((None, d, LANES), lambda j, m, a, b: (layer, 0, b[j]))],
        out_specs=pl.BlockSpec((d, LANES), lambda j, m, a, b: (0, j)),
    )
    return pl.pallas_call(
        _w_in_prep_kernel,
        grid_spec=grid_spec,
        out_shape=jax.ShapeDtypeStruct((d, W_ALL_COLS), BF16),
        compiler_params=_params("arbitrary"),
    )(mode, sa, sb, w, w)


def _cast_kernel(*refs):
    n = len(refs) // 2
    for x_ref, o_ref in zip(refs[:n], refs[n:]):
        o_ref[...] = x_ref[...].astype(o_ref.dtype)


def cast_experts_bf16(ws, layer, parts=2):
    e = ws[0].shape[1]
    in_specs, out_specs, out_shape = [], [], []
    for w in ws:
        _, _, k, n = w.shape
        in_specs.append(pl.BlockSpec((None, None, k // parts, n), lambda i, r: (layer, i, r, 0)))
        out_specs.append(pl.BlockSpec((None, k // parts, n), lambda i, r: (i, r, 0)))
        out_shape.append(jax.ShapeDtypeStruct((e, k, n), BF16))
    return pl.pallas_call(
        _cast_kernel,
        grid=(e, parts),
        in_specs=in_specs,
        out_specs=out_specs,
        out_shape=out_shape,
        compiler_params=_params("parallel", "parallel"),
    )(*ws)


def _proj_kernel(*refs, half):
    if half is None:
        h_ref, w_ref, o_ref = refs
        o_ref[...] = jnp.dot(h_ref[...], w_ref[...],
                             preferred_element_type=F32).astype(o_ref.dtype)
        return
    h_ref, w_ref, cs_ref, cos_ref, sup_ref, sdn_ref, o_ref = refs
    acc = jnp.dot(h_ref[...], w_ref[...], preferred_element_type=F32) * cs_ref[...]
    cos, s_up, s_dn = cos_ref[...], sup_ref[...], sdn_ref[...]
    for c in range(acc.shape[1] // LANES):
        sl = slice(c * LANES, (c + 1) * LANES)
        o_ref[:, sl] = _rope_tile(acc[:, sl], cos, s_up, s_dn, half).astype(o_ref.dtype)


def in_projection(h, w_all, col0, n, seq, rope=None):
    t, d = h.shape
    tm = _tile(seq, 1024)
    tn = PROJ_TN
    j0 = col0 // tn
    pos_blocks = seq // tm
    in_specs = [pl.BlockSpec((tm, d), lambda i, j: (i, 0)),
                pl.BlockSpec((d, tn), lambda i, j: (0, j0 + j))]
    args = [h, w_all]
    half = None
    if rope is not None:
        half, colscale, cos, s_up, s_dn = rope
        in_specs.append(pl.BlockSpec((1, tn), lambda i, j: (0, j)))
        tab = pl.BlockSpec((tm, LANES), lambda i, j: (i % pos_blocks, 0))
        in_specs += [tab, tab, tab]
        args += [colscale, cos, s_up, s_dn]
    return pl.pallas_call(
        functools.partial(_proj_kernel, half=half),
        grid=(t // tm, n // tn),
        in_specs=in_specs,
        out_specs=pl.BlockSpec((tm, tn), lambda i, j: (i, j)),
        out_shape=jax.ShapeDtypeStruct((t, n), BF16),
        compiler_params=_params("parallel", "parallel"),
    )(*args)


def _mla_proj_kernel(cq_ref, ckv_ref, kpe_ref, qn_ref, kvn_ref, wuq_ref, wukv_ref,
                     cos_ref, sup_ref, sdn_ref, q_ref, k_ref, v_ref, *, scale):
    cos, s_up, s_dn = cos_ref[...], sup_ref[...], sdn_ref[...]

    def normed(ref, g_ref):
        x = ref[...].astype(F32)
        ms = jnp.mean(x * x, axis=-1, keepdims=True)
        return (x * lax.rsqrt(ms + NORM_EPS) * g_ref[...]).astype(BF16)

    q = jnp.dot(normed(cq_ref, qn_ref), wuq_ref[...], preferred_element_type=F32) * scale
    kv = jnp.dot(normed(ckv_ref, kvn_ref), wukv_ref[...], preferred_element_type=F32)
    kpe = _rope_tile(kpe_ref[...].astype(F32), cos, s_up, s_dn, A_ROPE // 2).astype(BF16)
    for h in range(A_HEADS):
        c0 = h * A_QK_PAD
        q_ref[:, c0:c0 + A_NOPE] = q[:, c0:c0 + A_NOPE].astype(BF16)
        q_ref[:, c0 + A_NOPE:c0 + A_QK_PAD] = _rope_tile(
            q[:, c0 + A_NOPE:c0 + A_QK_PAD], cos, s_up, s_dn, A_ROPE // 2).astype(BF16)
        k_ref[:, c0:c0 + A_NOPE] = kv[:, h * A_NOPE:(h + 1) * A_NOPE].astype(BF16)
        k_ref[:, c0 + A_NOPE:c0 + A_QK_PAD] = kpe
    v_ref[...] = kv[:, A_HEADS * A_NOPE:].astype(BF16)


def mla_projection(cols_n, q_norm, kv_norm, w_uq, w_ukv, tabs64, seq):
    t = cols_n.shape[0]
    tm = _tile(seq, 512)
    pos_blocks = seq // tm
    qw = A_HEADS * A_QK_PAD
    vw = A_HEADS * A_V
    tab = pl.BlockSpec((tm, LANES), lambda i: (i % pos_blocks, 0))
    full = lambda shape: pl.BlockSpec(shape, lambda i: (0, 0))
    return pl.pallas_call(
        functools.partial(_mla_proj_kernel, scale=(A_NOPE + A_ROPE) ** -0.5 * LOG2E),
        grid=(t // tm,),
        in_specs=[pl.BlockSpec((tm, A_Q_RANK), lambda i: (i, N_CQ // A_Q_RANK)),
                  pl.BlockSpec((tm, A_KV_RANK), lambda i: (i, N_CKV // A_KV_RANK)),
                  pl.BlockSpec((tm, LANES), lambda i: (i, N_KPE // LANES)),
                  full((1, A_Q_RANK)), full((1, A_KV_RANK)),
                  full((A_Q_RANK, qw)), full((A_KV_RANK, qw)),
                  tab, tab, tab],
        out_specs=[pl.BlockSpec((tm, qw), lambda i: (i, 0)),
                   pl.BlockSpec((tm, qw), lambda i: (i, 0)),
                   pl.BlockSpec((tm, vw), lambda i: (i, 0))],
        out_shape=[jax.ShapeDtypeStruct((t, qw), BF16),
                   jax.ShapeDtypeStruct((t, qw), BF16),
                   jax.ShapeDtypeStruct((t, vw), BF16)],
        compiler_params=_params("parallel"),
    )(cols_n, cols_n, cols_n, q_norm.reshape(1, -1), kv_norm.reshape(1, -1),
      w_uq, w_ukv, *tabs64)


def _softmax_pv(s, v):
    m = jnp.max(s, axis=-1, keepdims=True)
    e = jnp.exp2(s - m)
    l = jnp.sum(e, axis=-1, keepdims=True)
    return jnp.dot(e.astype(BF16), v, preferred_element_type=F32) / l


def _full_attn_kernel(q_ref, k_ref, v_ref, o_ref):
    heads = o_ref.shape[1] // A_V
    s = [_dot_nt(q_ref[:, h * A_QK_PAD:(h + 1) * A_QK_PAD], k_ref[:, h * A_QK_PAD:(h + 1) * A_QK_PAD])
         for h in range(heads)]
    for h in range(heads):
        sl = slice(h * A_V, (h + 1) * A_V)
        o_ref[:, sl] = _softmax_pv(s[h], v_ref[:, sl]).astype(o_ref.dtype)


def mla_attention(q, k, v, nseq, seq):
    tq = _tile(seq, 256)
    nq = seq // tq
    hp = HEADS_PER_STEP
    return pl.pallas_call(
        _full_attn_kernel,
        grid=(nseq, A_HEADS // hp, nq),
        in_specs=[pl.BlockSpec((tq, hp * A_QK_PAD), lambda b, h, i: (b * nq + i, h)),
                  pl.BlockSpec((seq, hp * A_QK_PAD), lambda b, h, i: (b, h)),
                  pl.BlockSpec((seq, hp * A_V), lambda b, h, i: (b, h))],
        out_specs=pl.BlockSpec((tq, hp * A_V), lambda b, h, i: (b * nq + i, h)),
        out_shape=jax.ShapeDtypeStruct((nseq * seq, A_HEADS * A_V), BF16),
        compiler_params=_params("parallel", "parallel", "parallel"),
    )(q, k, v)


def _diff_attn_kernel(q_ref, k_ref, v_ref, lv_ref, g_ref, o_ref, *, lambda_init):
    lv = lv_ref[...]
    lam = (jnp.exp(jnp.sum(lv[0:1] * lv[1:2], axis=-1, keepdims=True))
           - jnp.exp(jnp.sum(lv[2:3] * lv[3:4], axis=-1, keepdims=True)) + lambda_init)
    lane = lax.broadcasted_iota(jnp.int32, (1, LANES), 1)
    heads = o_ref.shape[1] // B_V_DIM
    s = []
    for h in range(heads):
        sl = slice(h * LANES, (h + 1) * LANES)
        q, k = q_ref[:, sl], k_ref[:, sl]
        zero = jnp.zeros_like(q)
        s.append(_dot_nt(jnp.where(lane < B_QK_DIM, q, zero), k))
        s.append(_dot_nt(jnp.where(lane >= B_QK_DIM, q, zero), k))
    for h in range(heads):
        sl = slice(h * B_V_DIM, (h + 1) * B_V_DIM)
        v = v_ref[:, sl]
        o = _softmax_pv(s[2 * h], v) - lam * _softmax_pv(s[2 * h + 1], v)
        ms = jnp.mean(o * o, axis=-1, keepdims=True)
        o = o * lax.rsqrt(ms + B_SUBLN_EPS) * g_ref[...] * (1.0 - lambda_init)
        o_ref[:, sl] = o.astype(o_ref.dtype)


def diff_attention(cols_r64, cols_n, lvec, subln_g, lambda_init, nseq, seq):
    tq = _tile(seq, 256)
    nq = seq // tq
    hp = HEADS_PER_STEP
    w = hp * LANES
    qb, kb, vb = R64_BQ // w, R64_BK // w, N_BV // w
    return pl.pallas_call(
        functools.partial(_diff_attn_kernel, lambda_init=lambda_init),
        grid=(nseq, B_HEADS // hp, nq),
        in_specs=[pl.BlockSpec((tq, w), lambda b, h, i: (b * nq + i, qb + h)),
                  pl.BlockSpec((seq, w), lambda b, h, i: (b, kb + h)),
                  pl.BlockSpec((seq, w), lambda b, h, i: (b, vb + h)),
                  pl.BlockSpec((8, LANES), lambda b, h, i: (0, 0)),
                  pl.BlockSpec((1, B_V_DIM), lambda b, h, i: (0, 0))],
        out_specs=pl.BlockSpec((tq, w), lambda b, h, i: (b * nq + i, h)),
        out_shape=jax.ShapeDtypeStruct((nseq * seq, B_HEADS * B_V_DIM), BF16),
        compiler_params=_params("parallel", "parallel", "parallel"),
    )(cols_r64, cols_r64, cols_n, lvec, subln_g.reshape(1, -1))


def _window_attn_kernel(*refs, length, half_window, has_sink, want_stat, hp, kv_group):
    refs = list(refs)
    sink_ref = refs.pop(0) if has_sink else None
    q_ref, k_ref, v_ref, o_ref = refs[:4]
    stat_ref = refs[4] if want_stat else None
    h0 = pl.program_id(1) * hp
    qb = min(LANES, length)
    kw = min(qb + 2 * half_window, length)
    lane = lax.broadcasted_iota(jnp.int32, (1, LANES), 1)
    ones = _ones_column(kw)

    if want_stat:
        @pl.when(h0 == 0)
        def _():
            stat_ref[...] = jnp.zeros_like(stat_ref)

    def body(i, carry):
        r0 = pl.multiple_of(i * qb, qb)
        ks = pl.multiple_of(jnp.clip(r0 - half_window, 0, length - kw), 64)
        rows, keys = pl.ds(r0, qb), pl.ds(ks, kw)
        qpos = r0 + lax.broadcasted_iota(jnp.int32, (qb, 1), 0)
        kpos = ks + lax.broadcasted_iota(jnp.int32, (1, kw), 1)
        valid = jnp.abs(kpos - qpos) <= half_window
        stat = stat_ref[rows, :] if want_stat else None
        for hh in range(hp):
            qc = slice(hh * HEAD_DIM, (hh + 1) * HEAD_DIM)
            kc = slice(hh // kv_group * HEAD_DIM, (hh // kv_group + 1) * HEAD_DIM)
            s = jnp.where(valid, _dot_nt(q_ref[rows, qc], k_ref[keys, kc]), NEG_INF)
            m = jnp.max(s, axis=-1, keepdims=True)
            if has_sink:
                m = jnp.maximum(m, sink_ref[h0 + hh])
            v = jnp.concatenate([v_ref[keys, kc], ones], axis=-1)
            o = jnp.dot(jnp.exp2(s - m).astype(BF16), v, preferred_element_type=F32)
            l = o[:, HEAD_DIM:HEAD_DIM + 1]
            if has_sink:
                l = l + jnp.exp2(sink_ref[h0 + hh] - m)
            o_ref[rows, qc] = (o[:, :HEAD_DIM] / l).astype(o_ref.dtype)
            if want_stat:
                stat = jnp.where(lane == h0 + hh, m + jnp.log2(l), stat)
        if want_stat:
            stat_ref[rows, :] = stat
        return carry

    n_blocks = length // qb
    lax.fori_loop(0, n_blocks, body, 0, unroll=min(2, n_blocks))


def window_attention(q_arr, q_blk, k_arr, k_blk, v_arr, v_blk, *, nseq, length,
                     heads, kv_group, half_window, sink=None, want_stat=False):
    hp = kv_group
    while hp < heads and length * 2 * hp <= WINDOW_BLOCK_ELEMS:
        hp *= 2
    kvp = hp // kv_group
    assert heads % hp == 0 and q_blk % hp == 0 and k_blk % kvp == 0 and v_blk % kvp == 0
    in_specs = [pl.BlockSpec((length, hp * HEAD_DIM), lambda b, h: (b, q_blk // hp + h)),
                pl.BlockSpec((length, kvp * HEAD_DIM), lambda b, h: (b, k_blk // kvp + h)),
                pl.BlockSpec((length, kvp * HEAD_DIM), lambda b, h: (b, v_blk // kvp + h))]
    args = [q_arr, k_arr, v_arr]
    if sink is not None:
        in_specs.insert(0, pl.BlockSpec(memory_space=pltpu.SMEM))
        args.insert(0, sink.astype(F32))
    out_specs = [pl.BlockSpec((length, hp * HEAD_DIM), lambda b, h: (b, h))]
    out_shape = [jax.ShapeDtypeStruct((nseq * length, heads * HEAD_DIM), BF16)]
    if want_stat:
        out_specs.append(pl.BlockSpec((length, LANES), lambda b, h: (b, 0)))
        out_shape.append(jax.ShapeDtypeStruct((nseq * length, LANES), F32))
    return pl.pallas_call(
        functools.partial(_window_attn_kernel, length=length, half_window=half_window,
                          has_sink=sink is not None, want_stat=want_stat, hp=hp, kv_group=kv_group),
        grid=(nseq, heads // hp),
        in_specs=in_specs,
        out_specs=out_specs,
        out_shape=out_shape,
        compiler_params=_params("parallel", "arbitrary"),
    )(*args)


def _merge_kernel(*refs):
    o_refs, s_refs, out_ref = refs[:C_GROUPS], refs[C_GROUPS:2 * C_GROUPS], refs[-1]
    lse = [r[...] for r in s_refs]
    m = functools.reduce(jnp.maximum, lse)
    w = [jnp.exp2(x - m) for x in lse]
    inv = 1.0 / functools.reduce(lambda a, b: a + b, w)
    w = [x * inv for x in w]
    for h in range(C_HEADS):
        sl = slice(h * HEAD_DIM, (h + 1) * HEAD_DIM)
        acc = w[0][:, h:h + 1] * o_refs[0][:, sl].astype(F32)
        for g in range(1, C_GROUPS):
            acc = acc + w[g][:, h:h + 1] * o_refs[g][:, sl].astype(F32)
        out_ref[:, sl] = acc.astype(out_ref.dtype)


def merge_groups(outs, stats):
    t, w = outs[0].shape
    tm = _tile(t, 512)
    return pl.pallas_call(
        _merge_kernel,
        grid=(t // tm,),
        in_specs=([pl.BlockSpec((tm, w), lambda i: (i, 0))] * C_GROUPS
                  + [pl.BlockSpec((tm, LANES), lambda i: (i, 0))] * C_GROUPS),
        out_specs=pl.BlockSpec((tm, w), lambda i: (i, 0)),
        out_shape=jax.ShapeDtypeStruct((t, w), BF16),
        compiler_params=_params("parallel"),
    )(*outs, *stats)


def _out_proj_kernel(a_ref, b_ref, c_ref, d_ref, w_ref, x_ref, o_ref):
    o = jnp.concatenate([a_ref[...], b_ref[...], c_ref[...], d_ref[...]], axis=-1)
    o_ref[...] = x_ref[...] + jnp.dot(o, w_ref[...], preferred_element_type=F32)


def out_projection(parts, w_o, x):
    t, d = x.shape
    tm = _tile(t, 1024)
    tn = _tile(d, 512)
    in_specs = [pl.BlockSpec((tm, p.shape[1]), lambda i, j: (i, 0)) for p in parts]
    in_specs += [pl.BlockSpec((w_o.shape[0], tn), lambda i, j: (0, j)),
                 pl.BlockSpec((tm, tn), lambda i, j: (i, j))]
    return pl.pallas_call(
        _out_proj_kernel,
        grid=(t // tm, d // tn),
        in_specs=in_specs,
        out_specs=pl.BlockSpec((tm, tn), lambda i, j: (i, j)),
        out_shape=jax.ShapeDtypeStruct((t, d), F32),
        compiler_params=_params("parallel", "parallel"),
    )(*parts, w_o, x)


def _router_kernel(x_ref, g_ref, whi_ref, wlo_ref, b_ref, h_ref, r_ref):
    x = x_ref[...]
    ms = jnp.mean(x * x, axis=-1, keepdims=True)
    hn = x * lax.rsqrt(ms + NORM_EPS) * g_ref[...]
    hi = hn.astype(BF16)
    h_ref[...] = hi
    lo = (hn - hi.astype(F32)).astype(BF16)
    w_hi = whi_ref[...]
    logits = (jnp.dot(hi, w_hi, preferred_element_type=F32)
              + jnp.dot(lo, w_hi, preferred_element_type=F32)
              + jnp.dot(hi, wlo_ref[...], preferred_element_type=F32)) + b_ref[...]
    lane = lax.broadcasted_iota(jnp.int32, logits.shape, 1)
    ninf = float("-inf")
    big = jnp.int32(LANES)
    gmask = (lane >= N_EXPERTS) & (lane < N_EXPERTS + N_GROUPS)
    gl = jnp.where(gmask, logits, ninf)
    gmax = jnp.max(gl, axis=-1, keepdims=True)
    g_idx = jnp.min(jnp.where(gl == gmax, lane - N_EXPERTS, big), axis=-1, keepdims=True)
    g_w = 1.0 / jnp.sum(jnp.exp(gl - gmax), axis=-1, keepdims=True)
    emask = (lane < N_EXPERTS) & ((lane // EXPERTS_PER_GROUP) == g_idx)
    el = jnp.where(emask, logits, ninf)
    v1 = jnp.max(el, axis=-1, keepdims=True)
    i1 = jnp.min(jnp.where(el == v1, lane, big), axis=-1, keepdims=True)
    el2 = jnp.where(lane == i1, ninf, el)
    v2 = jnp.max(el2, axis=-1, keepdims=True)
    i2 = jnp.min(jnp.where(el2 == v2, lane, big), axis=-1, keepdims=True)
    t = jnp.exp(v2 - v1)
    w1 = g_w / (1.0 + t)
    w2 = w1 * t
    r_ref[...] = jnp.where(lane == 0, i1.astype(F32),
                           jnp.where(lane == 1, i2.astype(F32),
                                     jnp.where(lane == 2, w1, jnp.where(lane == 3, w2, 0.0))))


def norm_and_route(x, g, w_hi, w_lo, bias):
    t, d = x.shape
    tm = _tile(t, 512)
    full = lambda shape: pl.BlockSpec(shape, lambda i: (0, 0))
    return pl.pallas_call(
        _router_kernel,
        grid=(t // tm,),
        in_specs=[pl.BlockSpec((tm, d), lambda i: (i, 0)), full((1, d)),
                  full((d, LANES)), full((d, LANES)), full((1, LANES))],
        out_specs=[pl.BlockSpec((tm, d), lambda i: (i, 0)),
                   pl.BlockSpec((tm, LANES), lambda i: (i, 0))],
        out_shape=[jax.ShapeDtypeStruct((t, d), BF16),
                   jax.ShapeDtypeStruct((t, LANES), F32)],
        compiler_params=_params("parallel"),
    )(x, g.reshape(1, d), w_hi, w_lo, bias)


def _expert_kernel(te_ref, tv_ref, x_ref, gate_ref, wg_ref, wu_ref, wd_ref, o_ref):
    j = pl.program_id(0)

    @pl.when(tv_ref[j] != 0)
    def _():
        x = x_ref[...]
        hg = jnp.dot(x, wg_ref[...], preferred_element_type=F32)
        hu = jnp.dot(x, wu_ref[...], preferred_element_type=F32)
        a = hg / (1.0 + jnp.exp(-hg)) * hu * gate_ref[...]
        o_ref[...] = jnp.dot(a.astype(BF16), wd_ref[...], preferred_element_type=F32)

    @pl.when(tv_ref[j] == 0)
    def _():
        o_ref[...] = jnp.zeros_like(o_ref)


def expert_ffn(xg, gate_rows, tile_expert, tile_valid, w_gate, w_up, w_down, tm):
    r, d = xg.shape
    ff = w_gate.shape[2]
    grid_spec = pltpu.PrefetchScalarGridSpec(
        num_scalar_prefetch=2,
        grid=(r // tm,),
        in_specs=[pl.BlockSpec((tm, d), lambda j, te, tv: (j, 0)),
                  pl.BlockSpec((tm, 1), lambda j, te, tv: (j, 0)),
                  pl.BlockSpec((None, d, ff), lambda j, te, tv: (te[j], 0, 0)),
                  pl.BlockSpec((None, d, ff), lambda j, te, tv: (te[j], 0, 0)),
                  pl.BlockSpec((None, ff, d), lambda j, te, tv: (te[j], 0, 0))],
        out_specs=pl.BlockSpec((tm, d), lambda j, te, tv: (j, 0)),
    )
    return pl.pallas_call(
        _expert_kernel,
        grid_spec=grid_spec,
        out_shape=jax.ShapeDtypeStruct((r, d), F32),
        compiler_params=_params("arbitrary"),
    )(tile_expert, tile_valid, xg, gate_rows, w_gate, w_up, w_down)


def _combine_kernel(x_ref, y1_ref, y2_ref, g_ref, o_ref, *, final):
    x = x_ref[...] + y1_ref[...] + y2_ref[...]
    if final:
        ms = jnp.mean(x * x, axis=-1, keepdims=True)
        x = x * lax.rsqrt(ms + NORM_EPS) * g_ref[...]
    o_ref[...] = x


def combine(x, y1, y2, g, final, row0=0, nrows=None):
    d = x.shape[1]
    nrows = x.shape[0] if nrows is None else nrows
    tm = _tile(math.gcd(nrows, row0) if row0 else nrows, 256)
    i0 = row0 // tm
    row = pl.BlockSpec((tm, d), lambda i: (i0 + i, 0))
    return pl.pallas_call(
        functools.partial(_combine_kernel, final=final),
        grid=(nrows // tm,),
        in_specs=[row, row, row, pl.BlockSpec((1, d), lambda i: (0, 0))],
        out_specs=pl.BlockSpec((tm, d), lambda i: (i, 0)),
        out_shape=jax.ShapeDtypeStruct((nrows, d), F32),
        compiler_params=_params("parallel"),
    )(x, y1, y2, g.reshape(1, d))


def dispatch_plan(route, tm):
    t = route.shape[0]
    e_flat = jnp.concatenate([route[:, 0], route[:, 1]]).astype(jnp.int32)
    g_flat = jnp.concatenate([route[:, 2], route[:, 3]])
    pair = jnp.arange(2 * t, dtype=jnp.int32)
    e_sorted, order = lax.sort((e_flat, pair), num_keys=1, is_stable=True)
    experts = jnp.arange(N_EXPERTS, dtype=jnp.int32)
    counts = jnp.sum((e_flat[:, None] == experts[None, :]).astype(jnp.int32), axis=0)
    start = jnp.cumsum(counts) - counts
    padded = (counts + tm - 1) // tm * tm
    pad_end = jnp.cumsum(padded)
    pad_start = pad_end - padded
    rows = 2 * t + N_EXPERTS * tm
    tile_start = jnp.arange(rows // tm, dtype=jnp.int32) * tm
    tile_valid = tile_start < pad_end[-1]
    tile_expert = jnp.minimum(
        jnp.sum((pad_end[None, :] <= tile_start[:, None]).astype(jnp.int32), axis=1), N_EXPERTS - 1)
    e_row = jnp.repeat(tile_expert, tm)
    k = jnp.arange(rows, dtype=jnp.int32) - pad_start[e_row]
    valid = (k < counts[e_row]) & jnp.repeat(tile_valid, tm)
    src_pair = order[jnp.clip(start[e_row] + k, 0, 2 * t - 1)]
    src_token = jnp.where(valid, src_pair % t, 0)
    gate_rows = jnp.where(valid, g_flat[src_pair], 0.0)
    dest_sorted = pad_start[e_sorted] + pair - start[e_sorted]
    _, pos = lax.sort((order, dest_sorted), num_keys=1)
    return (src_token, gate_rows.reshape(rows, 1), tile_expert,
            tile_valid.astype(jnp.int32), pos[:t], pos[t:])


def rope_tables(seq, dim):
    inv = 1.0 / (ROPE_THETA ** (jnp.arange(0, dim, 2, dtype=F32) / dim))
    ang = jnp.arange(seq, dtype=F32)[:, None] * inv[None, :]
    cos, sin = jnp.cos(ang), jnp.sin(ang)
    reps = LANES // dim
    zero = jnp.zeros_like(sin)
    cos_t = jnp.tile(jnp.concatenate([cos, cos], axis=1), (1, reps))
    if reps == 1:
        return cos_t, jnp.concatenate([-sin, sin], axis=1), zero
    s_up = jnp.tile(jnp.concatenate([-sin, zero], axis=1), (1, reps))
    s_dn = jnp.tile(jnp.concatenate([zero, sin], axis=1), (1, reps))
    return cos_t, s_up, s_dn


def _pad_cols(w, n):
    return jnp.pad(w, ((0, 0), (0, n - w.shape[1])))


def split_mla_weights(w_uq, w_ukv):
    d = A_NOPE + A_ROPE
    wq = w_uq.reshape(A_Q_RANK, A_HEADS, d)
    wq = jnp.pad(wq, ((0, 0), (0, 0), (0, A_QK_PAD - d))).reshape(A_Q_RANK, A_HEADS * A_QK_PAD)
    wkv = w_ukv.reshape(A_KV_RANK, A_HEADS, A_NOPE + A_V)
    wkv = jnp.concatenate([wkv[:, :, :A_NOPE].reshape(A_KV_RANK, -1),
                           wkv[:, :, A_NOPE:].reshape(A_KV_RANK, -1)], axis=1)
    return wq.astype(BF16), wkv.astype(BF16)


def _residue_major(a, nseq, seq, dil):
    if dil == 1:
        return a
    w = a.shape[1]
    return a.reshape(nseq, seq // dil, dil, w).transpose(0, 2, 1, 3).reshape(nseq * seq, w)


def _natural_order(a, nseq, seq, dil):
    if dil == 1:
        return a
    w = a.shape[1]
    return a.reshape(nseq, dil, seq // dil, w).transpose(0, 2, 1, 3).reshape(nseq * seq, w)


def kernel(x_prompt, x_sample, attn_norm, w_in, mla_q_norm, mla_w_uq, mla_kv_norm, mla_w_ukv, diff_lq1, diff_lk1, diff_lq2, diff_lk2, diff_subln, win_sink, w_o, ffn_norm, router_group_w, router_group_b, router_expert_w, router_expert_b, expert_w_gate, expert_w_up, expert_w_down, final_norm):
    bp, seq, d = x_prompt.shape
    bs = x_sample.shape[0]
    assert x_sample.shape[1:] == (seq, d)
    nseq = bp + bs
    depth = w_in.shape[0]
    tabs64 = rope_tables(seq, A_ROPE)
    tabs128 = rope_tables(seq, HEAD_DIM)
    hscale = HEAD_DIM ** -0.5 * LOG2E
    cs64 = jnp.concatenate([jnp.full((R64_BK,), B_QK_DIM ** -0.5 * LOG2E, F32),
                            jnp.ones((R64_BK,), F32)]).reshape(1, -1)
    cs128 = jnp.concatenate([jnp.full((R128_CK,), hscale, F32), jnp.ones((R128_CK,), F32),
                             jnp.full((R128_DK - R128_DQ,), hscale, F32),
                             jnp.ones((W_R128_COLS - R128_DK,), F32)]).reshape(1, -1)
    cw = C_HEADS * HEAD_DIM
    tm_e = 256

    x = None
    for i in range(depth):
        lambda_init = 0.8 - 0.6 * math.exp(-0.3 * i)
        w_all = prepare_w_in(w_in, i)
        wq, wkv = split_mla_weights(mla_w_uq[i], mla_w_ukv[i])

        if i == 0:
            h, x = stack_and_norm(x_prompt.reshape(bp * seq, d), x_sample.reshape(bs * seq, d),
                                  attn_norm[i])
        else:
            h = rmsnorm_bf16(x, attn_norm[i])
        cols_n = in_projection(h, w_all, 0, W_N_COLS, seq)
        cols_r64 = in_projection(h, w_all, W_N_COLS, R64_COLS, seq,
                                 rope=(B_QK_DIM // 2, cs64, *tabs64))
        cols_r128 = in_projection(h, w_all, W_N_COLS + R64_COLS, W_R128_COLS, seq,
                                  rope=(HEAD_DIM // 2, cs128, *tabs128))

        qa, ka, va = mla_projection(cols_n, mla_q_norm[i], mla_kv_norm[i], wq, wkv, tabs64, seq)
        o_a = mla_attention(qa, ka, va, nseq, seq)

        lvec = jnp.zeros((8, LANES), F32).at[:4, :B_QK_DIM].set(
            jnp.stack([diff_lq1[i], diff_lk1[i], diff_lq2[i], diff_lk2[i]]))
        o_b = diff_attention(cols_r64, cols_n, lvec, diff_subln[i], lambda_init, nseq, seq)

        outs, stats = [], []
        for g, (window, dil) in enumerate(C_PATTERNS):
            qg = _residue_major(cols_r128[:, R128_CQ + g * cw:R128_CQ + (g + 1) * cw], nseq, seq, dil)
            kg = _residue_major(cols_r128[:, R128_CK + g * cw:R128_CK + (g + 1) * cw], nseq, seq, dil)
            vg = _residue_major(cols_n[:, N_CV + g * cw:N_CV + (g + 1) * cw], nseq, seq, dil)
            og, sg = window_attention(qg, 0, kg, 0, vg, 0, nseq=nseq * dil, length=seq // dil,
                                      heads=C_HEADS, kv_group=1,
                                      half_window=window // (2 * dil), want_stat=True)
            outs.append(_natural_order(og, nseq, seq, dil))
            stats.append(_natural_order(sg, nseq, seq, dil))
        o_c = merge_groups(outs, stats)

        (o_d,) = window_attention(cols_r128, R128_DQ // LANES, cols_r128, R128_DK // LANES,
                                  cols_n, N_DV // LANES, nseq=nseq, length=seq,
                                  heads=D_HEADS, kv_group=D_HEADS // D_KV_HEADS,
                                  half_window=D_WINDOW, sink=win_sink[i] * LOG2E)

        x = out_projection([o_a, o_b, o_c, o_d], w_o[i].astype(BF16), x)

        w_r = jnp.concatenate([router_expert_w[i].reshape(d, N_EXPERTS), router_group_w[i]], axis=1)
        w_r = _pad_cols(w_r, LANES)
        w_r_hi = w_r.astype(BF16)
        w_r_lo = (w_r - w_r_hi.astype(F32)).astype(BF16)
        b_r = jnp.concatenate([router_expert_b[i].reshape(-1), router_group_b[i]])
        b_r = jnp.pad(b_r, (0, LANES - b_r.shape[0])).reshape(1, LANES).astype(F32)
        h2, route = norm_and_route(x, ffn_norm[i], w_r_hi, w_r_lo, b_r)

        src_token, gate_rows, tile_expert, tile_valid, pos1, pos2 = dispatch_plan(route, tm_e)
        xg = h2.at[src_token].get(mode="promise_in_bounds")
        wg, wu, wd = cast_experts_bf16([expert_w_gate, expert_w_up, expert_w_down], i)
        yg = expert_ffn(xg, gate_rows, tile_expert, tile_valid, wg, wu, wd, tm_e)
        y1 = yg.at[pos1].get(mode="promise_in_bounds")
        y2 = yg.at[pos2].get(mode="promise_in_bounds")
        if i < depth - 1:
            x = combine(x, y1, y2, final_norm, final=False)
    y_p = combine(x, y1, y2, final_norm, final=True, row0=0, nrows=bp * seq)
    y_s = combine(x, y1, y2, final_norm, final=True, row0=bp * seq, nrows=bs * seq)
    return (y_p.reshape(bp, seq, d), y_s.reshape(bs, seq, d))
```

```python
import functools
import math

import jax
import jax.numpy as jnp
from jax import lax
from jax.experimental import pallas as pl
from jax.experimental.pallas import tpu as pltpu

F32 = jnp.float32
BF16 = jnp.bfloat16

HEAD_DIM = 128
ROPE_THETA = 10000.0
NORM_EPS = 1e-6
NEG_INF = -1e30
LOG2E = math.log2(math.e)

A_HEADS = 8
A_Q_RANK = 1536
A_KV_RANK = 512
A_NOPE = 128
A_ROPE = 64
A_V = 128
A_QK_PAD = 256

B_HEADS = 8
B_QK_DIM = 64
B_V_DIM = 128
B_SUBLN_EPS = 1e-5

C_PATTERNS = ((128, 1), (512, 4), (2048, 16))
C_GROUPS = len(C_PATTERNS)
C_HEADS = 8

D_HEADS = 8
D_KV_HEADS = 2
D_WINDOW = 128

N_GROUPS = 8
EXPERTS_PER_GROUP = 8
N_EXPERTS = N_GROUPS * EXPERTS_PER_GROUP

LANES = 128
VMEM_LIMIT = 56 * 1024 * 1024
HEADS_PER_STEP = 2
WINDOW_BLOCK_ELEMS = 8192

N_CQ, N_CKV, N_KPE = 0, A_Q_RANK, A_Q_RANK + A_KV_RANK
N_BV = N_KPE + 2 * LANES
N_CV = N_BV + B_HEADS * B_V_DIM
N_DV = N_CV + C_GROUPS * C_HEADS * HEAD_DIM
N_COLS = N_DV + D_KV_HEADS * HEAD_DIM
R64_BQ, R64_BK = 0, B_HEADS * 2 * B_QK_DIM
R64_COLS = 2 * R64_BK
R128_CQ = 0
R128_CK = C_GROUPS * C_HEADS * HEAD_DIM
R128_DQ = 2 * R128_CK
R128_DK = R128_DQ + D_HEADS * HEAD_DIM
R128_COLS = R128_DK + D_KV_HEADS * HEAD_DIM
PROJ_TN = 512
W_N_COLS = -(-N_COLS // PROJ_TN) * PROJ_TN
W_R128_COLS = -(-R128_COLS // PROJ_TN) * PROJ_TN
W_ALL_COLS = W_N_COLS + R64_COLS + W_R128_COLS


def _tile(n, pref):
    if n <= pref:
        return n
    t = pref
    while n % t:
        t //= 2
    return t


def _params(*sem):
    return pltpu.CompilerParams(dimension_semantics=sem, vmem_limit_bytes=VMEM_LIMIT)


def _dot_nt(a, b):
    return lax.dot_general(a, b, (((1,), (1,)), ((), ())), preferred_element_type=F32)


def _ones_column(rows):
    lane = lax.broadcasted_iota(jnp.int32, (rows, LANES), 1)
    return jnp.where(lane == 0, 1.0, 0.0).astype(BF16)


def _rope_tile(x, cos, s_up, s_dn, half):
    if 2 * half == LANES:
        return x * cos + pltpu.roll(x, half, 1) * s_up
    return (x * cos + pltpu.roll(x, LANES - half, 1) * s_up
            + pltpu.roll(x, half, 1) * s_dn)


def _rmsnorm_kernel(x_ref, g_ref, o_ref):
    x = x_ref[...]
    ms = jnp.mean(x * x, axis=-1, keepdims=True)
    o_ref[...] = (x * lax.rsqrt(ms + NORM_EPS) * g_ref[...]).astype(o_ref.dtype)


def rmsnorm_bf16(x, g):
    t, d = x.shape
    tm = _tile(t, 512)
    return pl.pallas_call(
        _rmsnorm_kernel,
        grid=(t // tm,),
        in_specs=[pl.BlockSpec((tm, d), lambda i: (i, 0)),
                  pl.BlockSpec((1, d), lambda i: (0, 0))],
        out_specs=pl.BlockSpec((tm, d), lambda i: (i, 0)),
        out_shape=jax.ShapeDtypeStruct((t, d), BF16),
        compiler_params=_params("parallel"),
    )(x, g.reshape(1, d))


def _stack_norm_kernel(xa_ref, xb_ref, g_ref, h_ref, x_ref, *, a_tiles):
    def emit(ref):
        x = ref[...]
        ms = jnp.mean(x * x, axis=-1, keepdims=True)
        h_ref[...] = (x * lax.rsqrt(ms + NORM_EPS) * g_ref[...]).astype(h_ref.dtype)
        x_ref[...] = x

    @pl.when(pl.program_id(0) < a_tiles)
    def _():
        emit(xa_ref)

    @pl.when(pl.program_id(0) >= a_tiles)
    def _():
        emit(xb_ref)


def stack_and_norm(xa, xb, g):
    d = xa.shape[1]
    tm = _tile(math.gcd(xa.shape[0], xb.shape[0]), 256)
    a_tiles, b_tiles = xa.shape[0] // tm, xb.shape[0] // tm
    t = xa.shape[0] + xb.shape[0]
    return pl.pallas_call(
        functools.partial(_stack_norm_kernel, a_tiles=a_tiles),
        grid=(a_tiles + b_tiles,),
        in_specs=[pl.BlockSpec((tm, d), lambda i: (jnp.minimum(i, a_tiles - 1), 0)),
                  pl.BlockSpec((tm, d), lambda i: (jnp.maximum(i - a_tiles, 0), 0)),
                  pl.BlockSpec((1, d), lambda i: (0, 0))],
        out_specs=[pl.BlockSpec((tm, d), lambda i: (i, 0)),
                   pl.BlockSpec((tm, d), lambda i: (i, 0))],
        out_shape=[jax.ShapeDtypeStruct((t, d), BF16), jax.ShapeDtypeStruct((t, d), F32)],
        compiler_params=_params("arbitrary"),
    )(xa, xb, g.reshape(1, d))


W_ZERO, W_COPY, W_SHIFT, W_LOW = 0, 1, 2, 3


def _w_in_plan():
    half = LANES // 2
    b0 = A_Q_RANK + A_KV_RANK + A_ROPE
    bw = B_HEADS * 2 * B_QK_DIM
    c0 = b0 + 2 * bw + B_HEADS * B_V_DIM
    cw = C_GROUPS * C_HEADS * HEAD_DIM
    d0 = c0 + 3 * cw
    dq, dk = D_HEADS * HEAD_DIM, D_KV_HEADS * HEAD_DIM
    plan = []

    def shifted(src, width):
        assert src % LANES == half and width % LANES == 0
        for t in range(width // LANES):
            a = (src - half) // LANES + t
            plan.append((W_SHIFT, a, a + 1))

    def pad_to(n):
        while len(plan) * LANES < n:
            plan.append((W_ZERO, 0, 0))

    assert b0 % LANES == half
    for t in range(b0 // LANES):
        plan.append((W_COPY, t, t))
    plan.append((W_LOW, b0 // LANES, b0 // LANES))
    pad_to(N_BV)
    shifted(b0 + 2 * bw, B_HEADS * B_V_DIM)
    shifted(c0 + 2 * cw, cw)
    shifted(d0 + dq + dk, dk)
    pad_to(W_N_COLS)
    shifted(b0, 2 * bw)
    pad_to(W_N_COLS + R64_COLS)
    shifted(c0, 2 * cw)
    shifted(d0, dq + dk)
    pad_to(W_ALL_COLS)
    return tuple(zip(*plan))


def _w_in_prep_kernel(mode_ref, sa_ref, sb_ref, a_ref, b_ref, o_ref):
    mode = mode_ref[pl.program_id(0)]
    low = lax.broadcasted_iota(jnp.int32, (1, LANES), 1) < LANES // 2

    @pl.when(mode == W_ZERO)
    def _():
        o_ref[...] = jnp.zeros_like(o_ref)

    @pl.when(mode == W_COPY)
    def _():
        o_ref[...] = a_ref[...].astype(o_ref.dtype)

    @pl.when(mode == W_LOW)
    def _():
        o_ref[...] = jnp.where(low, a_ref[...], 0.0).astype(o_ref.dtype)

    @pl.when(mode == W_SHIFT)
    def _():
        o_ref[...] = jnp.where(low, pltpu.roll(a_ref[...], LANES // 2, 1),
                               pltpu.roll(b_ref[...], LANES // 2, 1)).astype(o_ref.dtype)


def prepare_w_in(w, layer):
    d = w.shape[1]
    mode, sa, sb = (jnp.asarray(v, jnp.int32) for v in _w_in_plan())
    grid_spec = pltpu.PrefetchScalarGridSpec(
        num_scalar_prefetch=3,
        grid=(W_ALL_COLS // LANES,),
        in_specs=[pl.BlockSpec((None, d, LANES), lambda j, m, a, b: (layer, 0, a[j])),
                  pl.BlockSpec((None, d, LANES), lambda j, m, a, b: (layer, 0, b[j]))],
        out_specs=pl.BlockSpec((d, LANES), lambda j, m, a, b: (0, j)),
    )
    return pl.pallas_call(
        _w_in_prep_kernel,
        grid_spec=grid_spec,
        out_shape=jax.ShapeDtypeStruct((d, W_ALL_COLS), BF16),
        compiler_params=_params("arbitrary"),
    )(mode, sa, sb, w, w)


def _cast_kernel(*refs):
    n = len(refs) // 2
    for x_ref, o_ref in zip(refs[:n], refs[n:]):
        o_ref[...] = x_ref[...].astype(o_ref.dtype)


def cast_experts_bf16(ws, layer, parts=2):
    e = ws[0].shape[1]
    in_specs, out_specs, out_shape = [], [], []
    for w in ws:
        _, _, k, n = w.shape
        in_specs.append(pl.BlockSpec((None, None, k // parts, n), lambda i, r: (layer, i, r, 0)))
        out_specs.append(pl.BlockSpec((None, k // parts, n), lambda i, r: (i, r, 0)))
        out_shape.append(jax.ShapeDtypeStruct((e, k, n), BF16))
    return pl.pallas_call(
        _cast_kernel,
        grid=(e, parts),
        in_specs=in_specs,
        out_specs=out_specs,
        out_shape=out_shape,
        compiler_params=_params("parallel", "parallel"),
    )(*ws)


def _proj_kernel(*refs, half):
    if half is None:
        h_ref, w_ref, o_ref = refs
        o_ref[...] = jnp.dot(h_ref[...], w_ref[...],
                             preferred_element_type=F32).astype(o_ref.dtype)
        return
    h_ref, w_ref, cs_ref, cos_ref, sup_ref, sdn_ref, o_ref = refs
    acc = jnp.dot(h_ref[...], w_ref[...], preferred_element_type=F32) * cs_ref[...]
    cos, s_up, s_dn = cos_ref[...], sup_ref[...], sdn_ref[...]
    for c in range(acc.shape[1] // LANES):
        sl = slice(c * LANES, (c + 1) * LANES)
        o_ref[:, sl] = _rope_tile(acc[:, sl], cos, s_up, s_dn, half).astype(o_ref.dtype)


def in_projection(h, w_all, col0, n, seq, rope=None):
    t, d = h.shape
    tm = _tile(seq, 1024)
    tn = PROJ_TN
    j0 = col0 // tn
    pos_blocks = seq // tm
    in_specs = [pl.BlockSpec((tm, d), lambda i, j: (i, 0)),
                pl.BlockSpec((d, tn), lambda i, j: (0, j0 + j))]
    args = [h, w_all]
    half = None
    if rope is not None:
        half, colscale, cos, s_up, s_dn = rope
        in_specs.append(pl.BlockSpec((1, tn), lambda i, j: (0, j)))
        tab = pl.BlockSpec((tm, LANES), lambda i, j: (i % pos_blocks, 0))
        in_specs += [tab, tab, tab]
        args += [colscale, cos, s_up, s_dn]
    return pl.pallas_call(
        functools.partial(_proj_kernel, half=half),
        grid=(t // tm, n // tn),
        in_specs=in_specs,
        out_specs=pl.BlockSpec((tm, tn), lambda i, j: (i, j)),
        out_shape=jax.ShapeDtypeStruct((t, n), BF16),
        compiler_params=_params("parallel", "parallel"),
    )(*args)


def _mla_proj_kernel(cq_ref, ckv_ref, kpe_ref, qn_ref, kvn_ref, wuq_ref, wukv_ref,
                     cos_ref, sup_ref, sdn_ref, q_ref, k_ref, v_ref, *, scale):
    cos, s_up, s_dn = cos_ref[...], sup_ref[...], sdn_ref[...]

    def normed(ref, g_ref):
        x = ref[...].astype(F32)
        ms = jnp.mean(x * x, axis=-1, keepdims=True)
        return (x * lax.rsqrt(ms + NORM_EPS) * g_ref[...]).astype(BF16)

    q = jnp.dot(normed(cq_ref, qn_ref), wuq_ref[...], preferred_element_type=F32) * scale
    kv = jnp.dot(normed(ckv_ref, kvn_ref), wukv_ref[...], preferred_element_type=F32)
    kpe = _rope_tile(kpe_ref[...].astype(F32), cos, s_up, s_dn, A_ROPE // 2).astype(BF16)
    for h in range(A_HEADS):
        c0 = h * A_QK_PAD
        q_ref[:, c0:c0 + A_NOPE] = q[:, c0:c0 + A_NOPE].astype(BF16)
        q_ref[:, c0 + A_NOPE:c0 + A_QK_PAD] = _rope_tile(
            q[:, c0 + A_NOPE:c0 + A_QK_PAD], cos, s_up, s_dn, A_ROPE // 2).astype(BF16)
        k_ref[:, c0:c0 + A_NOPE] = kv[:, h * A_NOPE:(h + 1) * A_NOPE].astype(BF16)
        k_ref[:, c0 + A_NOPE:c0 + A_QK_PAD] = kpe
    v_ref[...] = kv[:, A_HEADS * A_NOPE:].astype(BF16)


def mla_projection(cols_n, q_norm, kv_norm, w_uq, w_ukv, tabs64, seq):
    t = cols_n.shape[0]
    tm = _tile(seq, 512)
    pos_blocks = seq // tm
    qw = A_HEADS * A_QK_PAD
    vw = A_HEADS * A_V
    tab = pl.BlockSpec((tm, LANES), lambda i: (i % pos_blocks, 0))
    full = lambda shape: pl.BlockSpec(shape, lambda i: (0, 0))
    return pl.pallas_call(
        functools.partial(_mla_proj_kernel, scale=(A_NOPE + A_ROPE) ** -0.5 * LOG2E),
        grid=(t // tm,),
        in_specs=[pl.BlockSpec((tm, A_Q_RANK), lambda i: (i, N_CQ // A_Q_RANK)),
                  pl.BlockSpec((tm, A_KV_RANK), lambda i: (i, N_CKV // A_KV_RANK)),
                  pl.BlockSpec((tm, LANES), lambda i: (i, N_KPE // LANES)),
                  full((1, A_Q_RANK)), full((1, A_KV_RANK)),
                  full((A_Q_RANK, qw)), full((A_KV_RANK, qw)),
                  tab, tab, tab],
        out_specs=[pl.BlockSpec((tm, qw), lambda i: (i, 0)),
                   pl.BlockSpec((tm, qw), lambda i: (i, 0)),
                   pl.BlockSpec((tm, vw), lambda i: (i, 0))],
        out_shape=[jax.ShapeDtypeStruct((t, qw), BF16),
                   jax.ShapeDtypeStruct((t, qw), BF16),
                   jax.ShapeDtypeStruct((t, vw), BF16)],
        compiler_params=_params("parallel"),
    )(cols_n, cols_n, cols_n, q_norm.reshape(1, -1), kv_norm.reshape(1, -1),
      w_uq, w_ukv, *tabs64)


def _softmax_pv(s, v):
    m = jnp.max(s, axis=-1, keepdims=True)
    e = jnp.exp2(s - m)
    l = jnp.sum(e, axis=-1, keepdims=True)
    return jnp.dot(e.astype(BF16), v, preferred_element_type=F32) / l


def _full_attn_kernel(q_ref, k_ref, v_ref, o_ref):
    heads = o_ref.shape[1] // A_V
    s = [_dot_nt(q_ref[:, h * A_QK_PAD:(h + 1) * A_QK_PAD], k_ref[:, h * A_QK_PAD:(h + 1) * A_QK_PAD])
         for h in range(heads)]
    for h in range(heads):
        sl = slice(h * A_V, (h + 1) * A_V)
        o_ref[:, sl] = _softmax_pv(s[h], v_ref[:, sl]).astype(o_ref.dtype)


def mla_attention(q, k, v, nseq, seq):
    tq = _tile(seq, 256)
    nq = seq // tq
    hp = HEADS_PER_STEP
    return pl.pallas_call(
        _full_attn_kernel,
        grid=(nseq, A_HEADS // hp, nq),
        in_specs=[pl.BlockSpec((tq, hp * A_QK_PAD), lambda b, h, i: (b * nq + i, h)),
                  pl.BlockSpec((seq, hp * A_QK_PAD), lambda b, h, i: (b, h)),
                  pl.BlockSpec((seq, hp * A_V), lambda b, h, i: (b, h))],
        out_specs=pl.BlockSpec((tq, hp * A_V), lambda b, h, i: (b * nq + i, h)),
        out_shape=jax.ShapeDtypeStruct((nseq * seq, A_HEADS * A_V), BF16),
        compiler_params=_params("parallel", "parallel", "parallel"),
    )(q, k, v)


def _diff_attn_kernel(q_ref, k_ref, v_ref, lv_ref, g_ref, o_ref, *, lambda_init):
    lv = lv_ref[...]
    lam = (jnp.exp(jnp.sum(lv[0:1] * lv[1:2], axis=-1, keepdims=True))
           - jnp.exp(jnp.sum(lv[2:3] * lv[3:4], axis=-1, keepdims=True)) + lambda_init)
    lane = lax.broadcasted_iota(jnp.int32, (1, LANES), 1)
    heads = o_ref.shape[1] // B_V_DIM
    s = []
    for h in range(heads):
        sl = slice(h * LANES, (h + 1) * LANES)
        q, k = q_ref[:, sl], k_ref[:, sl]
        zero = jnp.zeros_like(q)
        s.append(_dot_nt(jnp.where(lane < B_QK_DIM, q, zero), k))
        s.append(_dot_nt(jnp.where(lane >= B_QK_DIM, q, zero), k))
    for h in range(heads):
        sl = slice(h * B_V_DIM, (h + 1) * B_V_DIM)
        v = v_ref[:, sl]
        o = _softmax_pv(s[2 * h], v) - lam * _softmax_pv(s[2 * h + 1], v)
        ms = jnp.mean(o * o, axis=-1, keepdims=True)
        o = o * lax.rsqrt(ms + B_SUBLN_EPS) * g_ref[...] * (1.0 - lambda_init)
        o_ref[:, sl] = o.astype(o_ref.dtype)


def diff_attention(cols_r64, cols_n, lvec, subln_g, lambda_init, nseq, seq):
    tq = _tile(seq, 256)
    nq = seq // tq
    hp = HEADS_PER_STEP
    w = hp * LANES
    qb, kb, vb = R64_BQ // w, R64_BK // w, N_BV // w
    return pl.pallas_call(
        functools.partial(_diff_attn_kernel, lambda_init=lambda_init),
        grid=(nseq, B_HEADS // hp, nq),
        in_specs=[pl.BlockSpec((tq, w), lambda b, h, i: (b * nq + i, qb + h)),
                  pl.BlockSpec((seq, w), lambda b, h, i: (b, kb + h)),
                  pl.BlockSpec((seq, w), lambda b, h, i: (b, vb + h)),
                  pl.BlockSpec((8, LANES), lambda b, h, i: (0, 0)),
                  pl.BlockSpec((1, B_V_DIM), lambda b, h, i: (0, 0))],
        out_specs=pl.BlockSpec((tq, w), lambda b, h, i: (b * nq + i, h)),
        out_shape=jax.ShapeDtypeStruct((nseq * seq, B_HEADS * B_V_DIM), BF16),
        compiler_params=_params("parallel", "parallel", "parallel"),
    )(cols_r64, cols_r64, cols_n, lvec, subln_g.reshape(1, -1))


def _window_attn_kernel(*refs, length, half_window, has_sink, want_stat, hp, kv_group):
    refs = list(refs)
    sink_ref = refs.pop(0) if has_sink else None
    q_ref, k_ref, v_ref, o_ref = refs[:4]
    stat_ref = refs[4] if want_stat else None
    h0 = pl.program_id(1) * hp
    qb = min(LANES, length)
    kw = min(qb + 2 * half_window, length)
    lane = lax.broadcasted_iota(jnp.int32, (1, LANES), 1)
    ones = _ones_column(kw)

    if want_stat:
        @pl.when(h0 == 0)
        def _():
            stat_ref[...] = jnp.zeros_like(stat_ref)

    def body(i, carry):
        r0 = pl.multiple_of(i * qb, qb)
        ks = pl.multiple_of(jnp.clip(r0 - half_window, 0, length - kw), 64)
        rows, keys = pl.ds(r0, qb), pl.ds(ks, kw)
        qpos = r0 + lax.broadcasted_iota(jnp.int32, (qb, 1), 0)
        kpos = ks + lax.broadcasted_iota(jnp.int32, (1, kw), 1)
        valid = jnp.abs(kpos - qpos) <= half_window
        stat = stat_ref[rows, :] if want_stat else None
        for hh in range(hp):
            qc = slice(hh * HEAD_DIM, (hh + 1) * HEAD_DIM)
            kc = slice(hh // kv_group * HEAD_DIM, (hh // kv_group + 1) * HEAD_DIM)
            s = jnp.where(valid, _dot_nt(q_ref[rows, qc], k_ref[keys, kc]), NEG_INF)
            m = jnp.max(s, axis=-1, keepdims=True)
            if has_sink:
                m = jnp.maximum(m, sink_ref[h0 + hh])
            v = jnp.concatenate([v_ref[keys, kc], ones], axis=-1)
            o = jnp.dot(jnp.exp2(s - m).astype(BF16), v, preferred_element_type=F32)
            l = o[:, HEAD_DIM:HEAD_DIM + 1]
            if has_sink:
                l = l + jnp.exp2(sink_ref[h0 + hh] - m)
            o_ref[rows, qc] = (o[:, :HEAD_DIM] / l).astype(o_ref.dtype)
            if want_stat:
                stat = jnp.where(lane == h0 + hh, m + jnp.log2(l), stat)
        if want_stat:
            stat_ref[rows, :] = stat
        return carry

    n_blocks = length // qb
    lax.fori_loop(0, n_blocks, body, 0, unroll=min(2, n_blocks))


def window_attention(q_arr, q_blk, k_arr, k_blk, v_arr, v_blk, *, nseq, length,
                     heads, kv_group, half_window, sink=None, want_stat=False):
    hp = kv_group
    while hp < heads and length * 2 * hp <= WINDOW_BLOCK_ELEMS:
        hp *= 2
    kvp = hp // kv_group
    assert heads % hp == 0 and q_blk % hp == 0 and k_blk % kvp == 0 and v_blk % kvp == 0
    in_specs = [pl.BlockSpec((length, hp * HEAD_DIM), lambda b, h: (b, q_blk // hp + h)),
                pl.BlockSpec((length, kvp * HEAD_DIM), lambda b, h: (b, k_blk // kvp + h)),
                pl.BlockSpec((length, kvp * HEAD_DIM), lambda b, h: (b, v_blk // kvp + h))]
    args = [q_arr, k_arr, v_arr]
    if sink is not None:
        in_specs.insert(0, pl.BlockSpec(memory_space=pltpu.SMEM))
        args.insert(0, sink.astype(F32))
    out_specs = [pl.BlockSpec((length, hp * HEAD_DIM), lambda b, h: (b, h))]
    out_shape = [jax.ShapeDtypeStruct((nseq * length, heads * HEAD_DIM), BF16)]
    if want_stat:
        out_specs.append(pl.BlockSpec((length, LANES), lambda b, h: (b, 0)))
        out_shape.append(jax.ShapeDtypeStruct((nseq * length, LANES), F32))
    return pl.pallas_call(
        functools.partial(_window_attn_kernel, length=length, half_window=half_window,
                          has_sink=sink is not None, want_stat=want_stat, hp=hp, kv_group=kv_group),
        grid=(nseq, heads // hp),
        in_specs=in_specs,
        out_specs=out_specs,
        out_shape=out_shape,
        compiler_params=_params("parallel", "arbitrary"),
    )(*args)


def _merge_kernel(*refs):
    o_refs, s_refs, out_ref = refs[:C_GROUPS], refs[C_GROUPS:2 * C_GROUPS], refs[-1]
    lse = [r[...] for r in s_refs]
    m = functools.reduce(jnp.maximum, lse)
    w = [jnp.exp2(x - m) for x in lse]
    inv = 1.0 / functools.reduce(lambda a, b: a + b, w)
    w = [x * inv for x in w]
    for h in range(C_HEADS):
        sl = slice(h * HEAD_DIM, (h + 1) * HEAD_DIM)
        acc = w[0][:, h:h + 1] * o_refs[0][:, sl].astype(F32)
        for g in range(1, C_GROUPS):
            acc = acc + w[g][:, h:h + 1] * o_refs[g][:, sl].astype(F32)
        out_ref[:, sl] = acc.astype(out_ref.dtype)


def merge_groups(outs, stats):
    t, w = outs[0].shape
    tm = _tile(t, 512)
    return pl.pallas_call(
        _merge_kernel,
        grid=(t // tm,),
        in_specs=([pl.BlockSpec((tm, w), lambda i: (i, 0))] * C_GROUPS
                  + [pl.BlockSpec((tm, LANES), lambda i: (i, 0))] * C_GROUPS),
        out_specs=pl.BlockSpec((tm, w), lambda i: (i, 0)),
        out_shape=jax.ShapeDtypeStruct((t, w), BF16),
        compiler_params=_params("parallel"),
    )(*outs, *stats)


def _out_proj_kernel(a_ref, b_ref, c_ref, d_ref, w_ref, x_ref, o_ref):
    o = jnp.concatenate([a_ref[...], b_ref[...], c_ref[...], d_ref[...]], axis=-1)
    o_ref[...] = x_ref[...] + jnp.dot(o, w_ref[...], preferred_element_type=F32)


def out_projection(parts, w_o, x):
    t, d = x.shape
    tm = _tile(t, 1024)
    tn = _tile(d, 512)
    in_specs = [pl.BlockSpec((tm, p.shape[1]), lambda i, j: (i, 0)) for p in parts]
    in_specs += [pl.BlockSpec((w_o.shape[0], tn), lambda i, j: (0, j)),
                 pl.BlockSpec((tm, tn), lambda i, j: (i, j))]
    return pl.pallas_call(
        _out_proj_kernel,
        grid=(t // tm, d // tn),
        in_specs=in_specs,
        out_specs=pl.BlockSpec((tm, tn), lambda i, j: (i, j)),
        out_shape=jax.ShapeDtypeStruct((t, d), F32),
        compiler_params=_params("parallel", "parallel"),
    )(*parts, w_o, x)


def _router_kernel(x_ref, g_ref, whi_ref, wlo_ref, b_ref, h_ref, r_ref, c_ref):
    x = x_ref[...]
    ms = jnp.mean(x * x, axis=-1, keepdims=True)
    hn = x * lax.rsqrt(ms + NORM_EPS) * g_ref[...]
    h_ref[...] = hn
    hi = hn.astype(BF16)
    lo = (hn - hi.astype(F32)).astype(BF16)
    w_hi = whi_ref[...]
    logits = (jnp.dot(hi, w_hi, preferred_element_type=F32)
              + jnp.dot(lo, w_hi, preferred_element_type=F32)
              + jnp.dot(hi, wlo_ref[...], preferred_element_type=F32)) + b_ref[...]
    lane = lax.broadcasted_iota(jnp.int32, logits.shape, 1)
    ninf = float("-inf")
    big = jnp.int32(LANES)
    gmask = (lane >= N_EXPERTS) & (lane < N_EXPERTS + N_GROUPS)
    gl = jnp.where(gmask, logits, ninf)
    gmax = jnp.max(gl, axis=-1, keepdims=True)
    g_idx = jnp.min(jnp.where(gl == gmax, lane - N_EXPERTS, big), axis=-1, keepdims=True)
    g_w = 1.0 / jnp.sum(jnp.exp(gl - gmax), axis=-1, keepdims=True)
    emask = (lane < N_EXPERTS) & ((lane // EXPERTS_PER_GROUP) == g_idx)
    el = jnp.where(emask, logits, ninf)
    v1 = jnp.max(el, axis=-1, keepdims=True)
    i1 = jnp.min(jnp.where(el == v1, lane, big), axis=-1, keepdims=True)
    el2 = jnp.where(lane == i1, ninf, el)
    v2 = jnp.max(el2, axis=-1, keepdims=True)
    i2 = jnp.min(jnp.where(el2 == v2, lane, big), axis=-1, keepdims=True)
    t = jnp.exp(v2 - v1)
    w1 = g_w / (1.0 + t)
    w2 = w1 * t
    r_ref[...] = jnp.where(lane == 0, i1.astype(F32),
                           jnp.where(lane == 1, i2.astype(F32),
                                     jnp.where(lane == 2, w1, jnp.where(lane == 3, w2, 0.0))))
    hits = jnp.where((lane == i1) | (lane == i2), 1.0, 0.0)
    sub = lax.broadcasted_iota(jnp.int32, c_ref.shape, 0)
    c_ref[...] = jnp.where(sub == 0, jnp.sum(hits, axis=0, keepdims=True), 0.0)


def norm_and_route(x, g, w_hi, w_lo, bias):
    t, d = x.shape
    tm = _tile(t, 256)
    full = lambda shape: pl.BlockSpec(shape, lambda i: (0, 0))
    h, route, cnt = pl.pallas_call(
        _router_kernel,
        grid=(t // tm,),
        in_specs=[pl.BlockSpec((tm, d), lambda i: (i, 0)), full((1, d)),
                  full((d, LANES)), full((d, LANES)), full((1, LANES))],
        out_specs=[pl.BlockSpec((tm, d), lambda i: (i, 0)),
                   pl.BlockSpec((tm, LANES), lambda i: (i, 0)),
                   pl.BlockSpec((8, LANES), lambda i: (i, 0))],
        out_shape=[jax.ShapeDtypeStruct((t, d), F32),
                   jax.ShapeDtypeStruct((t, LANES), F32),
                   jax.ShapeDtypeStruct((t // tm * 8, LANES), F32)],
        compiler_params=_params("parallel"),
    )(x, g.reshape(1, d), w_hi, w_lo, bias)
    counts = jnp.sum(cnt, axis=0)[:N_EXPERTS].astype(jnp.int32)
    return h, route, counts


def _expert_kernel(te_ref, tv_ref, src_ref, gate_ref, h_hbm, wg_ref, wu_ref, wd_ref, o_ref,
                   xbuf, sem, *, tm):
    j = pl.program_id(0)
    last = pl.num_programs(0) - 1
    slot = j % 2

    def row_copy(tile, r, s):
        tok = src_ref[tile * tm + r]
        return pltpu.make_async_copy(h_hbm.at[pl.ds(tok, 1), :], xbuf.at[s, pl.ds(r, 1), :],
                                     sem.at[s])

    def start_tile(tile, s):
        for r in range(tm):
            row_copy(tile, r, s).start()

    @pl.when((j == 0) & (tv_ref[0] != 0))
    def _():
        start_tile(0, 0)

    nxt = jnp.minimum(j + 1, last)

    @pl.when((j < last) & (tv_ref[nxt] != 0))
    def _():
        start_tile(nxt, 1 - slot)

    @pl.when(tv_ref[j] != 0)
    def _():
        for r in range(tm):
            row_copy(j, r, slot).wait()
        x = xbuf[slot].astype(BF16)
        hg = jnp.dot(x, wg_ref[...], preferred_element_type=F32)
        hu = jnp.dot(x, wu_ref[...], preferred_element_type=F32)
        a = hg / (1.0 + jnp.exp(-hg)) * hu * gate_ref[...]
        o_ref[...] = jnp.dot(a.astype(BF16), wd_ref[...],
                             preferred_element_type=F32).astype(o_ref.dtype)

    @pl.when(tv_ref[j] == 0)
    def _():
        o_ref[...] = jnp.zeros_like(o_ref)


def expert_ffn(h, src_token, gate_rows, tile_expert, tile_valid, w_gate, w_up, w_down, tm):
    d = h.shape[1]
    r = src_token.shape[0]
    ff = w_gate.shape[2]
    grid_spec = pltpu.PrefetchScalarGridSpec(
        num_scalar_prefetch=3,
        grid=(r // tm,),
        in_specs=[pl.BlockSpec((tm, 1), lambda j, te, tv, src: (j, 0)),
                  pl.BlockSpec(memory_space=pl.ANY),
                  pl.BlockSpec((None, d, ff), lambda j, te, tv, src: (te[j], 0, 0)),
                  pl.BlockSpec((None, d, ff), lambda j, te, tv, src: (te[j], 0, 0)),
                  pl.BlockSpec((None, ff, d), lambda j, te, tv, src: (te[j], 0, 0))],
        out_specs=pl.BlockSpec((tm, d), lambda j, te, tv, src: (j, 0)),
        scratch_shapes=[pltpu.VMEM((2, tm, d), F32), pltpu.SemaphoreType.DMA((2,))],
    )
    return pl.pallas_call(
        functools.partial(_expert_kernel, tm=tm),
        grid_spec=grid_spec,
        out_shape=jax.ShapeDtypeStruct((r, d), BF16),
        compiler_params=_params("arbitrary"),
    )(tile_expert, tile_valid, src_token, gate_rows, h, w_gate, w_up, w_down)


def _combine_kernel(x_ref, y1_ref, y2_ref, g_ref, o_ref, *, final):
    x = x_ref[...] + y1_ref[...] + y2_ref[...]
    if final:
        ms = jnp.mean(x * x, axis=-1, keepdims=True)
        x = x * lax.rsqrt(ms + NORM_EPS) * g_ref[...]
    o_ref[...] = x


def combine(x, y1, y2, g, final, row0=0, nrows=None):
    d = x.shape[1]
    nrows = x.shape[0] if nrows is None else nrows
    tm = _tile(math.gcd(nrows, row0) if row0 else nrows, 256)
    i0 = row0 // tm
    row = pl.BlockSpec((tm, d), lambda i: (i0 + i, 0))
    return pl.pallas_call(
        functools.partial(_combine_kernel, final=final),
        grid=(nrows // tm,),
        in_specs=[row, row, row, pl.BlockSpec((1, d), lambda i: (0, 0))],
        out_specs=pl.BlockSpec((tm, d), lambda i: (i, 0)),
        out_shape=jax.ShapeDtypeStruct((nrows, d), F32),
        compiler_params=_params("parallel"),
    )(x, y1, y2, g.reshape(1, d))


def dispatch_plan(route, counts, tm):
    t = route.shape[0]
    e_flat = jnp.concatenate([route[:, 0], route[:, 1]]).astype(jnp.int32)
    g_flat = jnp.concatenate([route[:, 2], route[:, 3]])
    pair = jnp.arange(2 * t, dtype=jnp.int32)
    e_sorted, order = lax.sort((e_flat, pair), num_keys=1, is_stable=True)
    start = jnp.cumsum(counts) - counts
    padded = (counts + tm - 1) // tm * tm
    pad_end = jnp.cumsum(padded)
    pad_start = pad_end - padded
    rows = 2 * t + N_EXPERTS * tm
    tile_start = jnp.arange(rows // tm, dtype=jnp.int32) * tm
    tile_valid = tile_start < pad_end[-1]
    tile_expert = jnp.minimum(
        jnp.sum((pad_end[None, :] <= tile_start[:, None]).astype(jnp.int32), axis=1), N_EXPERTS - 1)
    e_row = jnp.repeat(tile_expert, tm)
    k = jnp.arange(rows, dtype=jnp.int32) - pad_start[e_row]
    valid = (k < counts[e_row]) & jnp.repeat(tile_valid, tm)
    src_pair = order[jnp.clip(start[e_row] + k, 0, 2 * t - 1)]
    src_token = jnp.where(valid, src_pair % t, 0)
    gate_rows = jnp.where(valid, g_flat[src_pair], 0.0)
    dest_sorted = pad_start[e_sorted] + pair - start[e_sorted]
    _, pos = lax.sort((order, dest_sorted), num_keys=1)
    return (src_token, gate_rows.reshape(rows, 1), tile_expert,
            tile_valid.astype(jnp.int32), pos[:t], pos[t:])


def rope_tables(seq, dim):
    inv = 1.0 / (ROPE_THETA ** (jnp.arange(0, dim, 2, dtype=F32) / dim))
    ang = jnp.arange(seq, dtype=F32)[:, None] * inv[None, :]
    cos, sin = jnp.cos(ang), jnp.sin(ang)
    reps = LANES // dim
    zero = jnp.zeros_like(sin)
    cos_t = jnp.tile(jnp.concatenate([cos, cos], axis=1), (1, reps))
    if reps == 1:
        return cos_t, jnp.concatenate([-sin, sin], axis=1), zero
    s_up = jnp.tile(jnp.concatenate([-sin, zero], axis=1), (1, reps))
    s_dn = jnp.tile(jnp.concatenate([zero, sin], axis=1), (1, reps))
    return cos_t, s_up, s_dn


def _pad_cols(w, n):
    return jnp.pad(w, ((0, 0), (0, n - w.shape[1])))


def split_mla_weights(w_uq, w_ukv):
    d = A_NOPE + A_ROPE
    wq = w_uq.reshape(A_Q_RANK, A_HEADS, d)
    wq = jnp.pad(wq, ((0, 0), (0, 0), (0, A_QK_PAD - d))).reshape(A_Q_RANK, A_HEADS * A_QK_PAD)
    wkv = w_ukv.reshape(A_KV_RANK, A_HEADS, A_NOPE + A_V)
    wkv = jnp.concatenate([wkv[:, :, :A_NOPE].reshape(A_KV_RANK, -1),
                           wkv[:, :, A_NOPE:].reshape(A_KV_RANK, -1)], axis=1)
    return wq.astype(BF16), wkv.astype(BF16)


def _residue_major(a, nseq, seq, dil):
    if dil == 1:
        return a
    w = a.shape[1]
    return a.reshape(nseq, seq // dil, dil, w).transpose(0, 2, 1, 3).reshape(nseq * seq, w)


def _natural_order(a, nseq, seq, dil):
    if dil == 1:
        return a
    w = a.shape[1]
    return a.reshape(nseq, dil, seq // dil, w).transpose(0, 2, 1, 3).reshape(nseq * seq, w)


def kernel(x_prompt, x_sample, attn_norm, w_in, mla_q_norm, mla_w_uq, mla_kv_norm, mla_w_ukv, diff_lq1, diff_lk1, diff_lq2, diff_lk2, diff_subln, win_sink, w_o, ffn_norm, router_group_w, router_group_b, router_expert_w, router_expert_b, expert_w_gate, expert_w_up, expert_w_down, final_norm):
    bp, seq, d = x_prompt.shape
    bs = x_sample.shape[0]
    assert x_sample.shape[1:] == (seq, d)
    nseq = bp + bs
    depth = w_in.shape[0]
    tabs64 = rope_tables(seq, A_ROPE)
    tabs128 = rope_tables(seq, HEAD_DIM)
    hscale = HEAD_DIM ** -0.5 * LOG2E
    cs64 = jnp.concatenate([jnp.full((R64_BK,), B_QK_DIM ** -0.5 * LOG2E, F32),
                            jnp.ones((R64_BK,), F32)]).reshape(1, -1)
    cs128 = jnp.concatenate([jnp.full((R128_CK,), hscale, F32), jnp.ones((R128_CK,), F32),
                             jnp.full((R128_DK - R128_DQ,), hscale, F32),
                             jnp.ones((W_R128_COLS - R128_DK,), F32)]).reshape(1, -1)
    cw = C_HEADS * HEAD_DIM
    tm_e = 256

    x = None
    for i in range(depth):
        lambda_init = 0.8 - 0.6 * math.exp(-0.3 * i)
        w_all = prepare_w_in(w_in, i)
        wq, wkv = split_mla_weights(mla_w_uq[i], mla_w_ukv[i])

        if i == 0:
            h, x = stack_and_norm(x_prompt.reshape(bp * seq, d), x_sample.reshape(bs * seq, d),
                                  attn_norm[i])
        else:
            h = rmsnorm_bf16(x, attn_norm[i])
        cols_n = in_projection(h, w_all, 0, W_N_COLS, seq)
        cols_r64 = in_projection(h, w_all, W_N_COLS, R64_COLS, seq,
                                 rope=(B_QK_DIM // 2, cs64, *tabs64))
        cols_r128 = in_projection(h, w_all, W_N_COLS + R64_COLS, W_R128_COLS, seq,
                                  rope=(HEAD_DIM // 2, cs128, *tabs128))

        qa, ka, va = mla_projection(cols_n, mla_q_norm[i], mla_kv_norm[i], wq, wkv, tabs64, seq)
        o_a = mla_attention(qa, ka, va, nseq, seq)

        lvec = jnp.zeros((8, LANES), F32).at[:4, :B_QK_DIM].set(
            jnp.stack([diff_lq1[i], diff_lk1[i], diff_lq2[i], diff_lk2[i]]))
        o_b = diff_attention(cols_r64, cols_n, lvec, diff_subln[i], lambda_init, nseq, seq)

        outs, stats = [], []
        for g, (window, dil) in enumerate(C_PATTERNS):
            qg = _residue_major(cols_r128[:, R128_CQ + g * cw:R128_CQ + (g + 1) * cw], nseq, seq, dil)
            kg = _residue_major(cols_r128[:, R128_CK + g * cw:R128_CK + (g + 1) * cw], nseq, seq, dil)
            vg = _residue_major(cols_n[:, N_CV + g * cw:N_CV + (g + 1) * cw], nseq, seq, dil)
            og, sg = window_attention(qg, 0, kg, 0, vg, 0, nseq=nseq * dil, length=seq // dil,
                                      heads=C_HEADS, kv_group=1,
                                      half_window=window // (2 * dil), want_stat=True)
            outs.append(_natural_order(og, nseq, seq, dil))
            stats.append(_natural_order(sg, nseq, seq, dil))
        o_c = merge_groups(outs, stats)

        (o_d,) = window_attention(cols_r128, R128_DQ // LANES, cols_r128, R128_DK // LANES,
                                  cols_n, N_DV // LANES, nseq=nseq, length=seq,
                                  heads=D_HEADS, kv_group=D_HEADS // D_KV_HEADS,
                                  half_window=D_WINDOW, sink=win_sink[i] * LOG2E)

        x = out_projection([o_a, o_b, o_c, o_d], w_o[i].astype(BF16), x)

        w_r = jnp.concatenate([router_expert_w[i].reshape(d, N_EXPERTS), router_group_w[i]], axis=1)
        w_r = _pad_cols(w_r, LANES)
        w_r_hi = w_r.astype(BF16)
        w_r_lo = (w_r - w_r_hi.astype(F32)).astype(BF16)
        b_r = jnp.concatenate([router_expert_b[i].reshape(-1), router_group_b[i]])
        b_r = jnp.pad(b_r, (0, LANES - b_r.shape[0])).reshape(1, LANES).astype(F32)
        h2, route, counts = norm_and_route(x, ffn_norm[i], w_r_hi, w_r_lo, b_r)

        src_token, gate_rows, tile_expert, tile_valid, pos1, pos2 = dispatch_plan(route, counts, tm_e)
        wg, wu, wd = cast_experts_bf16([expert_w_gate, expert_w_up, expert_w_down], i)
        yg = expert_ffn(h2, src_token, gate_rows, tile_expert, tile_valid, wg, wu, wd, tm_e)
        y1 = yg.at[pos1].get(mode="promise_in_bounds")
        y2 = yg.at[pos2].get(mode="promise_in_bounds")
        if i < depth - 1:
            x = combine(x, y1, y2, final_norm, final=False)
    y_p = combine(x, y1, y2, final_norm, final=True, row0=0, nrows=bp * seq)
    y_s = combine(x, y1, y2, final_norm, final=True, row0=bp * seq, nrows=bs * seq)
    return (y_p.reshape(bp, seq, d), y_s.reshape(bs, seq, d))
```

```python
import functools
import math

import jax
import jax.numpy as jnp
from jax import lax
from jax.experimental import pallas as pl
from jax.experimental.pallas import tpu as pltpu

F32 = jnp.float32
BF16 = jnp.bfloat16

HEAD_DIM = 128
ROPE_THETA = 10000.0
NORM_EPS = 1e-6
NEG_INF = -1e30
LOG2E = math.log2(math.e)

A_HEADS = 8
A_Q_RANK = 1536
A_KV_RANK = 512
A_NOPE = 128
A_ROPE = 64
A_V = 128
A_QK_PAD = 256

B_HEADS = 8
B_QK_DIM = 64
B_V_DIM = 128
B_SUBLN_EPS = 1e-5

C_PATTERNS = ((128, 1), (512, 4), (2048, 16))
C_GROUPS = len(C_PATTERNS)
C_HEADS = 8

D_HEADS = 8
D_KV_HEADS = 2
D_WINDOW = 128

N_GROUPS = 8
EXPERTS_PER_GROUP = 8
N_EXPERTS = N_GROUPS * EXPERTS_PER_GROUP

LANES = 128
VMEM_LIMIT = 56 * 1024 * 1024
HEADS_PER_STEP = 2
WINDOW_BLOCK_ELEMS = 8192

N_CQ, N_CKV, N_KPE = 0, A_Q_RANK, A_Q_RANK + A_KV_RANK
N_BV = N_KPE + 2 * LANES
N_CV = N_BV + B_HEADS * B_V_DIM
N_DV = N_CV + C_GROUPS * C_HEADS * HEAD_DIM
N_COLS = N_DV + D_KV_HEADS * HEAD_DIM
R64_BQ, R64_BK = 0, B_HEADS * 2 * B_QK_DIM
R64_COLS = 2 * R64_BK
R128_CQ = 0
R128_CK = C_GROUPS * C_HEADS * HEAD_DIM
R128_DQ = 2 * R128_CK
R128_DK = R128_DQ + D_HEADS * HEAD_DIM
R128_COLS = R128_DK + D_KV_HEADS * HEAD_DIM
PROJ_TN = 512
W_N_COLS = -(-N_COLS // PROJ_TN) * PROJ_TN
W_R128_COLS = -(-R128_COLS // PROJ_TN) * PROJ_TN
W_ALL_COLS = W_N_COLS + R64_COLS + W_R128_COLS


def _tile(n, pref):
    if n <= pref:
        return n
    t = pref
    while n % t:
        t //= 2
    return t


def _params(*sem):
    return pltpu.CompilerParams(dimension_semantics=sem, vmem_limit_bytes=VMEM_LIMIT)


def _dot_nt(a, b):
    return lax.dot_general(a, b, (((1,), (1,)), ((), ())), preferred_element_type=F32)


def _ones_column(rows):
    lane = lax.broadcasted_iota(jnp.int32, (rows, LANES), 1)
    return jnp.where(lane == 0, 1.0, 0.0).astype(BF16)


def _rope_tile(x, cos, s_up, s_dn, half):
    if 2 * half == LANES:
        return x * cos + pltpu.roll(x, half, 1) * s_up
    return (x * cos + pltpu.roll(x, LANES - half, 1) * s_up
            + pltpu.roll(x, half, 1) * s_dn)


def _rmsnorm_kernel(x_ref, g_ref, o_ref):
    x = x_ref[...]
    ms = jnp.mean(x * x, axis=-1, keepdims=True)
    o_ref[...] = (x * lax.rsqrt(ms + NORM_EPS) * g_ref[...]).astype(o_ref.dtype)


def rmsnorm_bf16(x, g):
    t, d = x.shape
    tm = _tile(t, 512)
    return pl.pallas_call(
        _rmsnorm_kernel,
        grid=(t // tm,),
        in_specs=[pl.BlockSpec((tm, d), lambda i: (i, 0)),
                  pl.BlockSpec((1, d), lambda i: (0, 0))],
        out_specs=pl.BlockSpec((tm, d), lambda i: (i, 0)),
        out_shape=jax.ShapeDtypeStruct((t, d), BF16),
        compiler_params=_params("parallel"),
    )(x, g.reshape(1, d))


def _stack_norm_kernel(xa_ref, xb_ref, g_ref, h_ref, x_ref, *, a_tiles):
    def emit(ref):
        x = ref[...]
        ms = jnp.mean(x * x, axis=-1, keepdims=True)
        h_ref[...] = (x * lax.rsqrt(ms + NORM_EPS) * g_ref[...]).astype(h_ref.dtype)
        x_ref[...] = x

    @pl.when(pl.program_id(0) < a_tiles)
    def _():
        emit(xa_ref)

    @pl.when(pl.program_id(0) >= a_tiles)
    def _():
        emit(xb_ref)


def stack_and_norm(xa, xb, g):
    d = xa.shape[1]
    tm = _tile(math.gcd(xa.shape[0], xb.shape[0]), 256)
    a_tiles, b_tiles = xa.shape[0] // tm, xb.shape[0] // tm
    t = xa.shape[0] + xb.shape[0]
    return pl.pallas_call(
        functools.partial(_stack_norm_kernel, a_tiles=a_tiles),
        grid=(a_tiles + b_tiles,),
        in_specs=[pl.BlockSpec((tm, d), lambda i: (jnp.minimum(i, a_tiles - 1), 0)),
                  pl.BlockSpec((tm, d), lambda i: (jnp.maximum(i - a_tiles, 0), 0)),
                  pl.BlockSpec((1, d), lambda i: (0, 0))],
        out_specs=[pl.BlockSpec((tm, d), lambda i: (i, 0)),
                   pl.BlockSpec((tm, d), lambda i: (i, 0))],
        out_shape=[jax.ShapeDtypeStruct((t, d), BF16), jax.ShapeDtypeStruct((t, d), F32)],
        compiler_params=_params("arbitrary"),
    )(xa, xb, g.reshape(1, d))


W_ZERO, W_COPY, W_SHIFT, W_LOW = 0, 1, 2, 3


def _w_in_plan():
    half = LANES // 2
    b0 = A_Q_RANK + A_KV_RANK + A_ROPE
    bw = B_HEADS * 2 * B_QK_DIM
    c0 = b0 + 2 * bw + B_HEADS * B_V_DIM
    cw = C_GROUPS * C_HEADS * HEAD_DIM
    d0 = c0 + 3 * cw
    dq, dk = D_HEADS * HEAD_DIM, D_KV_HEADS * HEAD_DIM
    plan = []

    def shifted(src, width):
        assert src % LANES == half and width % LANES == 0
        for t in range(width // LANES):
            a = (src - half) // LANES + t
            plan.append((W_SHIFT, a, a + 1))

    def pad_to(n):
        while len(plan) * LANES < n:
            plan.append((W_ZERO, 0, 0))

    assert b0 % LANES == half
    for t in range(b0 // LANES):
        plan.append((W_COPY, t, t))
    plan.append((W_LOW, b0 // LANES, b0 // LANES))
    pad_to(N_BV)
    shifted(b0 + 2 * bw, B_HEADS * B_V_DIM)
    shifted(c0 + 2 * cw, cw)
    shifted(d0 + dq + dk, dk)
    pad_to(W_N_COLS)
    shifted(b0, 2 * bw)
    pad_to(W_N_COLS + R64_COLS)
    shifted(c0, 2 * cw)
    shifted(d0, dq + dk)
    pad_to(W_ALL_COLS)
    return tuple(zip(*plan))


def _w_in_prep_kernel(mode_ref, sa_ref, sb_ref, a_ref, b_ref, o_ref):
    mode = mode_ref[pl.program_id(0)]
    low = lax.broadcasted_iota(jnp.int32, (1, LANES), 1) < LANES // 2

    @pl.when(mode == W_ZERO)
    def _():
        o_ref[...] = jnp.zeros_like(o_ref)

    @pl.when(mode == W_COPY)
    def _():
        o_ref[...] = a_ref[...].astype(o_ref.dtype)

    @pl.when(mode == W_LOW)
    def _():
        o_ref[...] = jnp.where(low, a_ref[...], 0.0).astype(o_ref.dtype)

    @pl.when(mode == W_SHIFT)
    def _():
        o_ref[...] = jnp.where(low, pltpu.roll(a_ref[...], LANES // 2, 1),
                               pltpu.roll(b_ref[...], LANES // 2, 1)).astype(o_ref.dtype)


def prepare_w_in(w, layer):
    d = w.shape[1]
    mode, sa, sb = (jnp.asarray(v, jnp.int32) for v in _w_in_plan())
    grid_spec = pltpu.PrefetchScalarGridSpec(
        num_scalar_prefetch=3,
        grid=(W_ALL_COLS // LANES,),
        in_specs=[pl.BlockSpec((None, d, LANES), lambda j, m, a, b: (layer, 0, a[j])),
                  pl.BlockSpec((None, d, LANES), lambda j, m, a, b: (layer, 0, b[j]))],
        out_specs=pl.BlockSpec((d, LANES), lambda j, m, a, b: (0, j)),
    )
    return pl.pallas_call(
        _w_in_prep_kernel,
        grid_spec=grid_spec,
        out_shape=jax.ShapeDtypeStruct((d, W_ALL_COLS), BF16),
        compiler_params=_params("arbitrary"),
    )(mode, sa, sb, w, w)


def _cast_kernel(*refs):
    n = len(refs) // 2
    for x_ref, o_ref in zip(refs[:n], refs[n:]):
        o_ref[...] = x_ref[...].astype(o_ref.dtype)


def cast_experts_bf16(ws, layer, parts=2):
    e = ws[0].shape[1]
    in_specs, out_specs, out_shape = [], [], []
    for w in ws:
        _, _, k, n = w.shape
        in_specs.append(pl.BlockSpec((None, None, k // parts, n), lambda i, r: (layer, i, r, 0)))
        out_specs.append(pl.BlockSpec((None, k // parts, n), lambda i, r: (i, r, 0)))
        out_shape.append(jax.ShapeDtypeStruct((e, k, n), BF16))
    return pl.pallas_call(
        _cast_kernel,
        grid=(e, parts),
        in_specs=in_specs,
        out_specs=out_specs,
        out_shape=out_shape,
        compiler_params=_params("parallel", "parallel"),
    )(*ws)


def _proj_kernel(*refs, half):
    if half is None:
        h_ref, w_ref, o_ref = refs
        o_ref[...] = jnp.dot(h_ref[...], w_ref[...],
                             preferred_element_type=F32).astype(o_ref.dtype)
        return
    h_ref, w_ref, cs_ref, cos_ref, sup_ref, sdn_ref, o_ref = refs
    acc = jnp.dot(h_ref[...], w_ref[...], preferred_element_type=F32) * cs_ref[...]
    cos, s_up, s_dn = cos_ref[...], sup_ref[...], sdn_ref[...]
    for c in range(acc.shape[1] // LANES):
        sl = slice(c * LANES, (c + 1) * LANES)
        o_ref[:, sl] = _rope_tile(acc[:, sl], cos, s_up, s_dn, half).astype(o_ref.dtype)


def in_projection(h, w_all, col0, n, seq, rope=None):
    t, d = h.shape
    tm = _tile(seq, 1024)
    tn = PROJ_TN
    j0 = col0 // tn
    pos_blocks = seq // tm
    in_specs = [pl.BlockSpec((tm, d), lambda i, j: (i, 0)),
                pl.BlockSpec((d, tn), lambda i, j: (0, j0 + j))]
    args = [h, w_all]
    half = None
    if rope is not None:
        half, colscale, cos, s_up, s_dn = rope
        in_specs.append(pl.BlockSpec((1, tn), lambda i, j: (0, j)))
        tab = pl.BlockSpec((tm, LANES), lambda i, j: (i % pos_blocks, 0))
        in_specs += [tab, tab, tab]
        args += [colscale, cos, s_up, s_dn]
    return pl.pallas_call(
        functools.partial(_proj_kernel, half=half),
        grid=(t // tm, n // tn),
        in_specs=in_specs,
        out_specs=pl.BlockSpec((tm, tn), lambda i, j: (i, j)),
        out_shape=jax.ShapeDtypeStruct((t, n), BF16),
        compiler_params=_params("parallel", "parallel"),
    )(*args)


def _mla_proj_kernel(cq_ref, ckv_ref, kpe_ref, qn_ref, kvn_ref, wuq_ref, wukv_ref,
                     cos_ref, sup_ref, sdn_ref, q_ref, k_ref, v_ref, *, scale):
    cos, s_up, s_dn = cos_ref[...], sup_ref[...], sdn_ref[...]

    def normed(ref, g_ref):
        x = ref[...].astype(F32)
        ms = jnp.mean(x * x, axis=-1, keepdims=True)
        return (x * lax.rsqrt(ms + NORM_EPS) * g_ref[...]).astype(BF16)

    q = jnp.dot(normed(cq_ref, qn_ref), wuq_ref[...], preferred_element_type=F32) * scale
    kv = jnp.dot(normed(ckv_ref, kvn_ref), wukv_ref[...], preferred_element_type=F32)
    kpe = _rope_tile(kpe_ref[...].astype(F32), cos, s_up, s_dn, A_ROPE // 2).astype(BF16)
    for h in range(A_HEADS):
        c0 = h * A_QK_PAD
        q_ref[:, c0:c0 + A_NOPE] = q[:, c0:c0 + A_NOPE].astype(BF16)
        q_ref[:, c0 + A_NOPE:c0 + A_QK_PAD] = _rope_tile(
            q[:, c0 + A_NOPE:c0 + A_QK_PAD], cos, s_up, s_dn, A_ROPE // 2).astype(BF16)
        k_ref[:, c0:c0 + A_NOPE] = kv[:, h * A_NOPE:(h + 1) * A_NOPE].astype(BF16)
        k_ref[:, c0 + A_NOPE:c0 + A_QK_PAD] = kpe
    v_ref[...] = kv[:, A_HEADS * A_NOPE:].astype(BF16)


def mla_projection(cols_n, q_norm, kv_norm, w_uq, w_ukv, tabs64, seq):
    t = cols_n.shape[0]
    tm = _tile(seq, 512)
    pos_blocks = seq // tm
    qw = A_HEADS * A_QK_PAD
    vw = A_HEADS * A_V
    tab = pl.BlockSpec((tm, LANES), lambda i: (i % pos_blocks, 0))
    full = lambda shape: pl.BlockSpec(shape, lambda i: (0, 0))
    return pl.pallas_call(
        functools.partial(_mla_proj_kernel, scale=(A_NOPE + A_ROPE) ** -0.5 * LOG2E),
        grid=(t // tm,),
        in_specs=[pl.BlockSpec((tm, A_Q_RANK), lambda i: (i, N_CQ // A_Q_RANK)),
                  pl.BlockSpec((tm, A_KV_RANK), lambda i: (i, N_CKV // A_KV_RANK)),
                  pl.BlockSpec((tm, LANES), lambda i: (i, N_KPE // LANES)),
                  full((1, A_Q_RANK)), full((1, A_KV_RANK)),
                  full((A_Q_RANK, qw)), full((A_KV_RANK, qw)),
                  tab, tab, tab],
        out_specs=[pl.BlockSpec((tm, qw), lambda i: (i, 0)),
                   pl.BlockSpec((tm, qw), lambda i: (i, 0)),
                   pl.BlockSpec((tm, vw), lambda i: (i, 0))],
        out_shape=[jax.ShapeDtypeStruct((t, qw), BF16),
                   jax.ShapeDtypeStruct((t, qw), BF16),
                   jax.ShapeDtypeStruct((t, vw), BF16)],
        compiler_params=_params("parallel"),
    )(cols_n, cols_n, cols_n, q_norm.reshape(1, -1), kv_norm.reshape(1, -1),
      w_uq, w_ukv, *tabs64)


def _softmax_pv(s, v):
    m = jnp.max(s, axis=-1, keepdims=True)
    e = jnp.exp2(s - m)
    l = jnp.sum(e, axis=-1, keepdims=True)
    return jnp.dot(e.astype(BF16), v, preferred_element_type=F32) / l


def _full_attn_kernel(q_ref, k_ref, v_ref, o_ref):
    heads = o_ref.shape[1] // A_V
    s = [_dot_nt(q_ref[:, h * A_QK_PAD:(h + 1) * A_QK_PAD], k_ref[:, h * A_QK_PAD:(h + 1) * A_QK_PAD])
         for h in range(heads)]
    for h in range(heads):
        sl = slice(h * A_V, (h + 1) * A_V)
        o_ref[:, sl] = _softmax_pv(s[h], v_ref[:, sl]).astype(o_ref.dtype)


def mla_attention(q, k, v, nseq, seq):
    tq = _tile(seq, 256)
    nq = seq // tq
    hp = HEADS_PER_STEP
    return pl.pallas_call(
        _full_attn_kernel,
        grid=(nseq, A_HEADS // hp, nq),
        in_specs=[pl.BlockSpec((tq, hp * A_QK_PAD), lambda b, h, i: (b * nq + i, h)),
                  pl.BlockSpec((seq, hp * A_QK_PAD), lambda b, h, i: (b, h)),
                  pl.BlockSpec((seq, hp * A_V), lambda b, h, i: (b, h))],
        out_specs=pl.BlockSpec((tq, hp * A_V), lambda b, h, i: (b * nq + i, h)),
        out_shape=jax.ShapeDtypeStruct((nseq * seq, A_HEADS * A_V), BF16),
        compiler_params=_params("parallel", "parallel", "parallel"),
    )(q, k, v)


def _diff_attn_kernel(q_ref, k_ref, v_ref, lv_ref, g_ref, o_ref, *, lambda_init):
    lv = lv_ref[...]
    lam = (jnp.exp(jnp.sum(lv[0:1] * lv[1:2], axis=-1, keepdims=True))
           - jnp.exp(jnp.sum(lv[2:3] * lv[3:4], axis=-1, keepdims=True)) + lambda_init)
    lane = lax.broadcasted_iota(jnp.int32, (1, LANES), 1)
    heads = o_ref.shape[1] // B_V_DIM
    s = []
    for h in range(heads):
        sl = slice(h * LANES, (h + 1) * LANES)
        q, k = q_ref[:, sl], k_ref[:, sl]
        zero = jnp.zeros_like(q)
        s.append(_dot_nt(jnp.where(lane < B_QK_DIM, q, zero), k))
        s.append(_dot_nt(jnp.where(lane >= B_QK_DIM, q, zero), k))
    for h in range(heads):
        sl = slice(h * B_V_DIM, (h + 1) * B_V_DIM)
        v = v_ref[:, sl]
        o = _softmax_pv(s[2 * h], v) - lam * _softmax_pv(s[2 * h + 1], v)
        ms = jnp.mean(o * o, axis=-1, keepdims=True)
        o = o * lax.rsqrt(ms + B_SUBLN_EPS) * g_ref[...] * (1.0 - lambda_init)
        o_ref[:, sl] = o.astype(o_ref.dtype)


def diff_attention(cols_r64, cols_n, lvec, subln_g, lambda_init, nseq, seq):
    tq = _tile(seq, 256)
    nq = seq // tq
    hp = HEADS_PER_STEP
    w = hp * LANES
    qb, kb, vb = R64_BQ // w, R64_BK // w, N_BV // w
    return pl.pallas_call(
        functools.partial(_diff_attn_kernel, lambda_init=lambda_init),
        grid=(nseq, B_HEADS // hp, nq),
        in_specs=[pl.BlockSpec((tq, w), lambda b, h, i: (b * nq + i, qb + h)),
                  pl.BlockSpec((seq, w), lambda b, h, i: (b, kb + h)),
                  pl.BlockSpec((seq, w), lambda b, h, i: (b, vb + h)),
                  pl.BlockSpec((8, LANES), lambda b, h, i: (0, 0)),
                  pl.BlockSpec((1, B_V_DIM), lambda b, h, i: (0, 0))],
        out_specs=pl.BlockSpec((tq, w), lambda b, h, i: (b * nq + i, h)),
        out_shape=jax.ShapeDtypeStruct((nseq * seq, B_HEADS * B_V_DIM), BF16),
        compiler_params=_params("parallel", "parallel", "parallel"),
    )(cols_r64, cols_r64, cols_n, lvec, subln_g.reshape(1, -1))


def _window_attn_kernel(*refs, length, half_window, has_sink, want_stat, hp, kv_group):
    refs = list(refs)
    sink_ref = refs.pop(0) if has_sink else None
    q_ref, k_ref, v_ref, o_ref = refs[:4]
    stat_ref = refs[4] if want_stat else None
    h0 = pl.program_id(1) * hp
    qb = min(LANES, length)
    kw = min(qb + 2 * half_window, length)
    lane = lax.broadcasted_iota(jnp.int32, (1, LANES), 1)
    ones = _ones_column(kw)

    if want_stat:
        @pl.when(h0 == 0)
        def _():
            stat_ref[...] = jnp.zeros_like(stat_ref)

    def body(i, carry):
        r0 = pl.multiple_of(i * qb, qb)
        ks = pl.multiple_of(jnp.clip(r0 - half_window, 0, length - kw), 64)
        rows, keys = pl.ds(r0, qb), pl.ds(ks, kw)
        qpos = r0 + lax.broadcasted_iota(jnp.int32, (qb, 1), 0)
        kpos = ks + lax.broadcasted_iota(jnp.int32, (1, kw), 1)
        valid = jnp.abs(kpos - qpos) <= half_window
        stat = stat_ref[rows, :] if want_stat else None
        for hh in range(hp):
            qc = slice(hh * HEAD_DIM, (hh + 1) * HEAD_DIM)
            kc = slice(hh // kv_group * HEAD_DIM, (hh // kv_group + 1) * HEAD_DIM)
            s = jnp.where(valid, _dot_nt(q_ref[rows, qc], k_ref[keys, kc]), NEG_INF)
            m = jnp.max(s, axis=-1, keepdims=True)
            if has_sink:
                m = jnp.maximum(m, sink_ref[h0 + hh])
            v = jnp.concatenate([v_ref[keys, kc], ones], axis=-1)
            o = jnp.dot(jnp.exp2(s - m).astype(BF16), v, preferred_element_type=F32)
            l = o[:, HEAD_DIM:HEAD_DIM + 1]
            if has_sink:
                l = l + jnp.exp2(sink_ref[h0 + hh] - m)
            o_ref[rows, qc] = (o[:, :HEAD_DIM] / l).astype(o_ref.dtype)
            if want_stat:
                stat = jnp.where(lane == h0 + hh, m + jnp.log2(l), stat)
        if want_stat:
            stat_ref[rows, :] = stat
        return carry

    n_blocks = length // qb
    lax.fori_loop(0, n_blocks, body, 0, unroll=min(2, n_blocks))


def window_attention(q_arr, q_blk, k_arr, k_blk, v_arr, v_blk, *, nseq, length,
                     heads, kv_group, half_window, sink=None, want_stat=False):
    hp = kv_group
    while hp < heads and length * 2 * hp <= WINDOW_BLOCK_ELEMS:
        hp *= 2
    kvp = hp // kv_group
    assert heads % hp == 0 and q_blk % hp == 0 and k_blk % kvp == 0 and v_blk % kvp == 0
    in_specs = [pl.BlockSpec((length, hp * HEAD_DIM), lambda b, h: (b, q_blk // hp + h)),
                pl.BlockSpec((length, kvp * HEAD_DIM), lambda b, h: (b, k_blk // kvp + h)),
                pl.BlockSpec((length, kvp * HEAD_DIM), lambda b, h: (b, v_blk // kvp + h))]
    args = [q_arr, k_arr, v_arr]
    if sink is not None:
        in_specs.insert(0, pl.BlockSpec(memory_space=pltpu.SMEM))
        args.insert(0, sink.astype(F32))
    out_specs = [pl.BlockSpec((length, hp * HEAD_DIM), lambda b, h: (b, h))]
    out_shape = [jax.ShapeDtypeStruct((nseq * length, heads * HEAD_DIM), BF16)]
    if want_stat:
        out_specs.append(pl.BlockSpec((length, LANES), lambda b, h: (b, 0)))
        out_shape.append(jax.ShapeDtypeStruct((nseq * length, LANES), F32))
    return pl.pallas_call(
        functools.partial(_window_attn_kernel, length=length, half_window=half_window,
                          has_sink=sink is not None, want_stat=want_stat, hp=hp, kv_group=kv_group),
        grid=(nseq, heads // hp),
        in_specs=in_specs,
        out_specs=out_specs,
        out_shape=out_shape,
        compiler_params=_params("parallel", "arbitrary"),
    )(*args)


def _merge_kernel(*refs):
    o_refs, s_refs, out_ref = refs[:C_GROUPS], refs[C_GROUPS:2 * C_GROUPS], refs[-1]
    lse = [r[...] for r in s_refs]
    m = functools.reduce(jnp.maximum, lse)
    w = [jnp.exp2(x - m) for x in lse]
    inv = 1.0 / functools.reduce(lambda a, b: a + b, w)
    w = [x * inv for x in w]
    for h in range(C_HEADS):
        sl = slice(h * HEAD_DIM, (h + 1) * HEAD_DIM)
        acc = w[0][:, h:h + 1] * o_refs[0][:, sl].astype(F32)
        for g in range(1, C_GROUPS):
            acc = acc + w[g][:, h:h + 1] * o_refs[g][:, sl].astype(F32)
        out_ref[:, sl] = acc.astype(out_ref.dtype)


def merge_groups(outs, stats):
    t, w = outs[0].shape
    tm = _tile(t, 512)
    return pl.pallas_call(
        _merge_kernel,
        grid=(t // tm,),
        in_specs=([pl.BlockSpec((tm, w), lambda i: (i, 0))] * C_GROUPS
                  + [pl.BlockSpec((tm, LANES), lambda i: (i, 0))] * C_GROUPS),
        out_specs=pl.BlockSpec((tm, w), lambda i: (i, 0)),
        out_shape=jax.ShapeDtypeStruct((t, w), BF16),
        compiler_params=_params("parallel"),
    )(*outs, *stats)


def _out_proj_kernel(a_ref, b_ref, c_ref, d_ref, w_ref, x_ref, o_ref):
    o = jnp.concatenate([a_ref[...], b_ref[...], c_ref[...], d_ref[...]], axis=-1)
    o_ref[...] = x_ref[...] + jnp.dot(o, w_ref[...], preferred_element_type=F32)


def out_projection(parts, w_o, x):
    t, d = x.shape
    tm = _tile(t, 1024)
    tn = _tile(d, 512)
    in_specs = [pl.BlockSpec((tm, p.shape[1]), lambda i, j: (i, 0)) for p in parts]
    in_specs += [pl.BlockSpec((w_o.shape[0], tn), lambda i, j: (0, j)),
                 pl.BlockSpec((tm, tn), lambda i, j: (i, j))]
    return pl.pallas_call(
        _out_proj_kernel,
        grid=(t // tm, d // tn),
        in_specs=in_specs,
        out_specs=pl.BlockSpec((tm, tn), lambda i, j: (i, j)),
        out_shape=jax.ShapeDtypeStruct((t, d), F32),
        compiler_params=_params("parallel", "parallel"),
    )(*parts, w_o, x)


def _router_kernel(x_ref, g_ref, whi_ref, wlo_ref, b_ref, h_ref, r_ref, c_ref):
    x = x_ref[...]
    ms = jnp.mean(x * x, axis=-1, keepdims=True)
    hn = x * lax.rsqrt(ms + NORM_EPS) * g_ref[...]
    h_ref[...] = hn
    hi = hn.astype(BF16)
    lo = (hn - hi.astype(F32)).astype(BF16)
    w_hi = whi_ref[...]
    logits = (jnp.dot(hi, w_hi, preferred_element_type=F32)
              + jnp.dot(lo, w_hi, preferred_element_type=F32)
              + jnp.dot(hi, wlo_ref[...], preferred_element_type=F32)) + b_ref[...]
    lane = lax.broadcasted_iota(jnp.int32, logits.shape, 1)
    ninf = float("-inf")
    big = jnp.int32(LANES)
    gmask = (lane >= N_EXPERTS) & (lane < N_EXPERTS + N_GROUPS)
    gl = jnp.where(gmask, logits, ninf)
    gmax = jnp.max(gl, axis=-1, keepdims=True)
    g_idx = jnp.min(jnp.where(gl == gmax, lane - N_EXPERTS, big), axis=-1, keepdims=True)
    g_w = 1.0 / jnp.sum(jnp.exp(gl - gmax), axis=-1, keepdims=True)
    emask = (lane < N_EXPERTS) & ((lane // EXPERTS_PER_GROUP) == g_idx)
    el = jnp.where(emask, logits, ninf)
    v1 = jnp.max(el, axis=-1, keepdims=True)
    i1 = jnp.min(jnp.where(el == v1, lane, big), axis=-1, keepdims=True)
    el2 = jnp.where(lane == i1, ninf, el)
    v2 = jnp.max(el2, axis=-1, keepdims=True)
    i2 = jnp.min(jnp.where(el2 == v2, lane, big), axis=-1, keepdims=True)
    t = jnp.exp(v2 - v1)
    w1 = g_w / (1.0 + t)
    w2 = w1 * t
    r_ref[...] = jnp.where(lane == 0, i1.astype(F32),
                           jnp.where(lane == 1, i2.astype(F32),
                                     jnp.where(lane == 2, w1, jnp.where(lane == 3, w2, 0.0))))
    hits = jnp.where((lane == i1) | (lane == i2), 1.0, 0.0)
    sub = lax.broadcasted_iota(jnp.int32, c_ref.shape, 0)
    c_ref[...] = jnp.where(sub == 0, jnp.sum(hits, axis=0, keepdims=True), 0.0)


def norm_and_route(x, g, w_hi, w_lo, bias):
    t, d = x.shape
    tm = _tile(t, 256)
    full = lambda shape: pl.BlockSpec(shape, lambda i: (0, 0))
    h, route, cnt = pl.pallas_call(
        _router_kernel,
        grid=(t // tm,),
        in_specs=[pl.BlockSpec((tm, d), lambda i: (i, 0)), full((1, d)),
                  full((d, LANES)), full((d, LANES)), full((1, LANES))],
        out_specs=[pl.BlockSpec((tm, d), lambda i: (i, 0)),
                   pl.BlockSpec((tm, LANES), lambda i: (i, 0)),
                   pl.BlockSpec((8, LANES), lambda i: (i, 0))],
        out_shape=[jax.ShapeDtypeStruct((t, d), F32),
                   jax.ShapeDtypeStruct((t, LANES), F32),
                   jax.ShapeDtypeStruct((t // tm * 8, LANES), F32)],
        compiler_params=_params("parallel"),
    )(x, g.reshape(1, d), w_hi, w_lo, bias)
    counts = jnp.sum(cnt, axis=0)[:N_EXPERTS].astype(jnp.int32)
    return h, route, counts


def _expert_kernel(te_ref, tv_ref, src_ref, gate_ref, h_hbm, wg_ref, wu_ref, wd_ref, o_ref,
                   xbuf, sem, *, tm):
    j = pl.program_id(0)
    last = pl.num_programs(0) - 1
    slot = j % 2

    def row_copy(tile, r, s):
        tok = src_ref[tile * tm + r]
        return pltpu.make_async_copy(h_hbm.at[pl.ds(tok, 1), :], xbuf.at[s, pl.ds(r, 1), :],
                                     sem.at[s])

    def start_tile(tile, s):
        for r in range(tm):
            row_copy(tile, r, s).start()

    @pl.when((j == 0) & (tv_ref[0] != 0))
    def _():
        start_tile(0, 0)

    prev_valid = tv_ref[jnp.maximum(j - 1, 0)] != 0

    @pl.when(prev_valid | ((j == 0) & (tv_ref[0] != 0)))
    def _():
        for r in range(tm):
            row_copy(j, r, slot).wait()

    @pl.when(tv_ref[j] != 0)
    def _():
        x = xbuf[slot].astype(BF16)
        start_tile(jnp.minimum(j + 1, last), 1 - slot)
        hg = jnp.dot(x, wg_ref[...], preferred_element_type=F32)
        hu = jnp.dot(x, wu_ref[...], preferred_element_type=F32)
        a = hg / (1.0 + jnp.exp(-hg)) * hu * gate_ref[...]
        o_ref[...] = jnp.dot(a.astype(BF16), wd_ref[...],
                             preferred_element_type=F32).astype(o_ref.dtype)

    @pl.when(tv_ref[j] == 0)
    def _():
        o_ref[...] = jnp.zeros_like(o_ref)


def expert_ffn(h, src_token, gate_rows, tile_expert, tile_valid, w_gate, w_up, w_down, tm):
    d = h.shape[1]
    r = src_token.shape[0]
    ff = w_gate.shape[2]
    grid_spec = pltpu.PrefetchScalarGridSpec(
        num_scalar_prefetch=3,
        grid=(r // tm,),
        in_specs=[pl.BlockSpec((tm, 1), lambda j, te, tv, src: (j, 0)),
                  pl.BlockSpec(memory_space=pl.ANY),
                  pl.BlockSpec((None, d, ff), lambda j, te, tv, src: (te[j], 0, 0)),
                  pl.BlockSpec((None, d, ff), lambda j, te, tv, src: (te[j], 0, 0)),
                  pl.BlockSpec((None, ff, d), lambda j, te, tv, src: (te[j], 0, 0))],
        out_specs=pl.BlockSpec((tm, d), lambda j, te, tv, src: (j, 0)),
        scratch_shapes=[pltpu.VMEM((2, tm, d), F32), pltpu.SemaphoreType.DMA((2,))],
    )
    return pl.pallas_call(
        functools.partial(_expert_kernel, tm=tm),
        grid_spec=grid_spec,
        out_shape=jax.ShapeDtypeStruct((r, d), BF16),
        compiler_params=pltpu.CompilerParams(dimension_semantics=("arbitrary",),
                                             vmem_limit_bytes=VMEM_LIMIT,
                                             disable_bounds_checks=True),
    )(tile_expert, tile_valid, src_token, gate_rows, h, w_gate, w_up, w_down)


def _combine_kernel(x_ref, y1_ref, y2_ref, g_ref, o_ref, *, final):
    x = x_ref[...] + y1_ref[...] + y2_ref[...]
    if final:
        ms = jnp.mean(x * x, axis=-1, keepdims=True)
        x = x * lax.rsqrt(ms + NORM_EPS) * g_ref[...]
    o_ref[...] = x


def combine(x, y1, y2, g, final, row0=0, nrows=None):
    d = x.shape[1]
    nrows = x.shape[0] if nrows is None else nrows
    tm = _tile(math.gcd(nrows, row0) if row0 else nrows, 256)
    i0 = row0 // tm
    row = pl.BlockSpec((tm, d), lambda i: (i0 + i, 0))
    return pl.pallas_call(
        functools.partial(_combine_kernel, final=final),
        grid=(nrows // tm,),
        in_specs=[row, row, row, pl.BlockSpec((1, d), lambda i: (0, 0))],
        out_specs=pl.BlockSpec((tm, d), lambda i: (i, 0)),
        out_shape=jax.ShapeDtypeStruct((nrows, d), F32),
        compiler_params=_params("parallel"),
    )(x, y1, y2, g.reshape(1, d))


def dispatch_plan(route, counts, tm):
    t = route.shape[0]
    e_flat = jnp.concatenate([route[:, 0], route[:, 1]]).astype(jnp.int32)
    g_flat = jnp.concatenate([route[:, 2], route[:, 3]])
    pair = jnp.arange(2 * t, dtype=jnp.int32)
    _, order = lax.sort((e_flat, pair), num_keys=1, is_stable=True)
    start = jnp.cumsum(counts) - counts
    padded = (counts + tm - 1) // tm * tm
    pad_end = jnp.cumsum(padded)
    pad_start = pad_end - padded
    n_tiles = 2 * t // tm + N_EXPERTS + 1
    rows = n_tiles * tm
    tile_start = jnp.arange(n_tiles, dtype=jnp.int32) * tm
    tile_valid = tile_start < pad_end[-1]
    tile_expert = jnp.minimum(
        jnp.sum((pad_end[None, :] <= tile_start[:, None]).astype(jnp.int32), axis=1), N_EXPERTS - 1)
    within = jnp.arange(tm, dtype=jnp.int32)[None, :]
    k = tile_start[:, None] + within - pad_start[tile_expert][:, None]
    valid = ((k < counts[tile_expert][:, None]) & tile_valid[:, None]).reshape(rows)
    sorted_idx = jnp.clip(start[tile_expert][:, None] + k, 0, 2 * t - 1).reshape(rows)
    src_pair = order[sorted_idx]
    src_token = jnp.where(valid, src_pair % t, 0)
    gate_rows = jnp.where(valid, g_flat[src_pair], 0.0)
    row = jnp.arange(rows, dtype=jnp.int32)
    _, pos = lax.sort((jnp.where(valid, src_pair, 2 * t + row), row), num_keys=1)
    return (src_token, gate_rows.reshape(rows, 1), tile_expert,
            tile_valid.astype(jnp.int32), pos[:t], pos[t:2 * t])


def rope_tables(seq, dim):
    inv = 1.0 / (ROPE_THETA ** (jnp.arange(0, dim, 2, dtype=F32) / dim))
    ang = jnp.arange(seq, dtype=F32)[:, None] * inv[None, :]
    cos, sin = jnp.cos(ang), jnp.sin(ang)
    reps = LANES // dim
    zero = jnp.zeros_like(sin)
    cos_t = jnp.tile(jnp.concatenate([cos, cos], axis=1), (1, reps))
    if reps == 1:
        return cos_t, jnp.concatenate([-sin, sin], axis=1), zero
    s_up = jnp.tile(jnp.concatenate([-sin, zero], axis=1), (1, reps))
    s_dn = jnp.tile(jnp.concatenate([zero, sin], axis=1), (1, reps))
    return cos_t, s_up, s_dn


def _pad_cols(w, n):
    return jnp.pad(w, ((0, 0), (0, n - w.shape[1])))


def split_mla_weights(w_uq, w_ukv):
    d = A_NOPE + A_ROPE
    wq = w_uq.reshape(A_Q_RANK, A_HEADS, d)
    wq = jnp.pad(wq, ((0, 0), (0, 0), (0, A_QK_PAD - d))).reshape(A_Q_RANK, A_HEADS * A_QK_PAD)
    wkv = w_ukv.reshape(A_KV_RANK, A_HEADS, A_NOPE + A_V)
    wkv = jnp.concatenate([wkv[:, :, :A_NOPE].reshape(A_KV_RANK, -1),
                           wkv[:, :, A_NOPE:].reshape(A_KV_RANK, -1)], axis=1)
    return wq.astype(BF16), wkv.astype(BF16)


def _residue_major(a, nseq, seq, dil):
    if dil == 1:
        return a
    w = a.shape[1]
    return a.reshape(nseq, seq // dil, dil, w).transpose(0, 2, 1, 3).reshape(nseq * seq, w)


def _natural_order(a, nseq, seq, dil):
    if dil == 1:
        return a
    w = a.shape[1]
    return a.reshape(nseq, dil, seq // dil, w).transpose(0, 2, 1, 3).reshape(nseq * seq, w)


def kernel(x_prompt, x_sample, attn_norm, w_in, mla_q_norm, mla_w_uq, mla_kv_norm, mla_w_ukv, diff_lq1, diff_lk1, diff_lq2, diff_lk2, diff_subln, win_sink, w_o, ffn_norm, router_group_w, router_group_b, router_expert_w, router_expert_b, expert_w_gate, expert_w_up, expert_w_down, final_norm):
    bp, seq, d = x_prompt.shape
    bs = x_sample.shape[0]
    assert x_sample.shape[1:] == (seq, d)
    nseq = bp + bs
    depth = w_in.shape[0]
    tabs64 = rope_tables(seq, A_ROPE)
    tabs128 = rope_tables(seq, HEAD_DIM)
    hscale = HEAD_DIM ** -0.5 * LOG2E
    cs64 = jnp.concatenate([jnp.full((R64_BK,), B_QK_DIM ** -0.5 * LOG2E, F32),
                            jnp.ones((R64_BK,), F32)]).reshape(1, -1)
    cs128 = jnp.concatenate([jnp.full((R128_CK,), hscale, F32), jnp.ones((R128_CK,), F32),
                             jnp.full((R128_DK - R128_DQ,), hscale, F32),
                             jnp.ones((W_R128_COLS - R128_DK,), F32)]).reshape(1, -1)
    cw = C_HEADS * HEAD_DIM
    tm_e = 256

    x = None
    for i in range(depth):
        lambda_init = 0.8 - 0.6 * math.exp(-0.3 * i)
        w_all = prepare_w_in(w_in, i)
        wq, wkv = split_mla_weights(mla_w_uq[i], mla_w_ukv[i])

        if i == 0:
            h, x = stack_and_norm(x_prompt.reshape(bp * seq, d), x_sample.reshape(bs * seq, d),
                                  attn_norm[i])
        else:
            h = rmsnorm_bf16(x, attn_norm[i])
        cols_n = in_projection(h, w_all, 0, W_N_COLS, seq)
        cols_r64 = in_projection(h, w_all, W_N_COLS, R64_COLS, seq,
                                 rope=(B_QK_DIM // 2, cs64, *tabs64))
        cols_r128 = in_projection(h, w_all, W_N_COLS + R64_COLS, W_R128_COLS, seq,
                                  rope=(HEAD_DIM // 2, cs128, *tabs128))

        qa, ka, va = mla_projection(cols_n, mla_q_norm[i], mla_kv_norm[i], wq, wkv, tabs64, seq)
        o_a = mla_attention(qa, ka, va, nseq, seq)

        lvec = jnp.zeros((8, LANES), F32).at[:4, :B_QK_DIM].set(
            jnp.stack([diff_lq1[i], diff_lk1[i], diff_lq2[i], diff_lk2[i]]))
        o_b = diff_attention(cols_r64, cols_n, lvec, diff_subln[i], lambda_init, nseq, seq)

        outs, stats = [], []
        for g, (window, dil) in enumerate(C_PATTERNS):
            qg = _residue_major(cols_r128[:, R128_CQ + g * cw:R128_CQ + (g + 1) * cw], nseq, seq, dil)
            kg = _residue_major(cols_r128[:, R128_CK + g * cw:R128_CK + (g + 1) * cw], nseq, seq, dil)
            vg = _residue_major(cols_n[:, N_CV + g * cw:N_CV + (g + 1) * cw], nseq, seq, dil)
            og, sg = window_attention(qg, 0, kg, 0, vg, 0, nseq=nseq * dil, length=seq // dil,
                                      heads=C_HEADS, kv_group=1,
                                      half_window=window // (2 * dil), want_stat=True)
            outs.append(_natural_order(og, nseq, seq, dil))
            stats.append(_natural_order(sg, nseq, seq, dil))
        o_c = merge_groups(outs, stats)

        (o_d,) = window_attention(cols_r128, R128_DQ // LANES, cols_r128, R128_DK // LANES,
                                  cols_n, N_DV // LANES, nseq=nseq, length=seq,
                                  heads=D_HEADS, kv_group=D_HEADS // D_KV_HEADS,
                                  half_window=D_WINDOW, sink=win_sink[i] * LOG2E)

        x = out_projection([o_a, o_b, o_c, o_d], w_o[i].astype(BF16), x)

        w_r = jnp.concatenate([router_expert_w[i].reshape(d, N_EXPERTS), router_group_w[i]], axis=1)
        w_r = _pad_cols(w_r, LANES)
        w_r_hi = w_r.astype(BF16)
        w_r_lo = (w_r - w_r_hi.astype(F32)).astype(BF16)
        b_r = jnp.concatenate([router_expert_b[i].reshape(-1), router_group_b[i]])
        b_r = jnp.pad(b_r, (0, LANES - b_r.shape[0])).reshape(1, LANES).astype(F32)
        h2, route, counts = norm_and_route(x, ffn_norm[i], w_r_hi, w_r_lo, b_r)

        src_token, gate_rows, tile_expert, tile_valid, pos1, pos2 = dispatch_plan(route, counts, tm_e)
        wg, wu, wd = cast_experts_bf16([expert_w_gate, expert_w_up, expert_w_down], i)
        yg = expert_ffn(h2, src_token, gate_rows, tile_expert, tile_valid, wg, wu, wd, tm_e)
        y1 = yg.at[pos1].get(mode="promise_in_bounds")
        y2 = yg.at[pos2].get(mode="promise_in_bounds")
        if i < depth - 1:
            x = combine(x, y1, y2, final_norm, final=False)
    y_p = combine(x, y1, y2, final_norm, final=True, row0=0, nrows=bp * seq)
    y_s = combine(x, y1, y2, final_norm, final=True, row0=bp * seq, nrows=bs * seq)
    return (y_p.reshape(bp, seq, d), y_s.reshape(bs, seq, d))
```

```python
import functools
import math

import jax
import jax.numpy as jnp
from jax import lax
from jax.experimental import pallas as pl
from jax.experimental.pallas import tpu as pltpu

F32 = jnp.float32
BF16 = jnp.bfloat16

HEAD_DIM = 128
ROPE_THETA = 10000.0
NORM_EPS = 1e-6
NEG_INF = -1e30
LOG2E = math.log2(math.e)

A_HEADS = 8
A_Q_RANK = 1536
A_KV_RANK = 512
A_NOPE = 128
A_ROPE = 64
A_V = 128
A_QK_PAD = 256

B_HEADS = 8
B_QK_DIM = 64
B_V_DIM = 128
B_SUBLN_EPS = 1e-5

C_PATTERNS = ((128, 1), (512, 4), (2048, 16))
C_GROUPS = len(C_PATTERNS)
C_HEADS = 8

D_HEADS = 8
D_KV_HEADS = 2
D_WINDOW = 128

N_GROUPS = 8
EXPERTS_PER_GROUP = 8
N_EXPERTS = N_GROUPS * EXPERTS_PER_GROUP

LANES = 128
VMEM_LIMIT = 56 * 1024 * 1024
HEADS_PER_STEP = 2
WINDOW_BLOCK_ELEMS = 8192

N_CQ, N_CKV, N_KPE = 0, A_Q_RANK, A_Q_RANK + A_KV_RANK
N_BV = N_KPE + 2 * LANES
N_CV = N_BV + B_HEADS * B_V_DIM
N_DV = N_CV + C_GROUPS * C_HEADS * HEAD_DIM
N_COLS = N_DV + D_KV_HEADS * HEAD_DIM
R64_BQ, R64_BK = 0, B_HEADS * 2 * B_QK_DIM
R64_COLS = 2 * R64_BK
R128_CQ = 0
R128_CK = C_GROUPS * C_HEADS * HEAD_DIM
R128_DQ = 2 * R128_CK
R128_DK = R128_DQ + D_HEADS * HEAD_DIM
R128_COLS = R128_DK + D_KV_HEADS * HEAD_DIM
PROJ_TN = 512
W_N_COLS = -(-N_COLS // PROJ_TN) * PROJ_TN
W_R128_COLS = -(-R128_COLS // PROJ_TN) * PROJ_TN
W_ALL_COLS = W_N_COLS + R64_COLS + W_R128_COLS


def _tile(n, pref):
    if n <= pref:
        return n
    t = pref
    while n % t:
        t //= 2
    return t


def _params(*sem):
    return pltpu.CompilerParams(dimension_semantics=sem, vmem_limit_bytes=VMEM_LIMIT)


def _dot_nt(a, b):
    return lax.dot_general(a, b, (((1,), (1,)), ((), ())), preferred_element_type=F32)


def _ones_column(rows):
    lane = lax.broadcasted_iota(jnp.int32, (rows, LANES), 1)
    return jnp.where(lane == 0, 1.0, 0.0).astype(BF16)


def _rope_tile(x, cos, s_up, s_dn, half):
    if 2 * half == LANES:
        return x * cos + pltpu.roll(x, half, 1) * s_up
    return (x * cos + pltpu.roll(x, LANES - half, 1) * s_up
            + pltpu.roll(x, half, 1) * s_dn)


def _rmsnorm_kernel(x_ref, g_ref, o_ref):
    x = x_ref[...]
    ms = jnp.mean(x * x, axis=-1, keepdims=True)
    o_ref[...] = (x * lax.rsqrt(ms + NORM_EPS) * g_ref[...]).astype(o_ref.dtype)


def rmsnorm_bf16(x, g):
    t, d = x.shape
    tm = _tile(t, 512)
    return pl.pallas_call(
        _rmsnorm_kernel,
        grid=(t // tm,),
        in_specs=[pl.BlockSpec((tm, d), lambda i: (i, 0)),
                  pl.BlockSpec((1, d), lambda i: (0, 0))],
        out_specs=pl.BlockSpec((tm, d), lambda i: (i, 0)),
        out_shape=jax.ShapeDtypeStruct((t, d), BF16),
        compiler_params=_params("parallel"),
    )(x, g.reshape(1, d))


def _stack_norm_kernel(xa_ref, xb_ref, g_ref, h_ref, x_ref, *, a_tiles):
    def emit(ref):
        x = ref[...]
        ms = jnp.mean(x * x, axis=-1, keepdims=True)
        h_ref[...] = (x * lax.rsqrt(ms + NORM_EPS) * g_ref[...]).astype(h_ref.dtype)
        x_ref[...] = x

    @pl.when(pl.program_id(0) < a_tiles)
    def _():
        emit(xa_ref)

    @pl.when(pl.program_id(0) >= a_tiles)
    def _():
        emit(xb_ref)


def stack_and_norm(xa, xb, g):
    d = xa.shape[1]
    tm = _tile(math.gcd(xa.shape[0], xb.shape[0]), 256)
    a_tiles, b_tiles = xa.shape[0] // tm, xb.shape[0] // tm
    t = xa.shape[0] + xb.shape[0]
    return pl.pallas_call(
        functools.partial(_stack_norm_kernel, a_tiles=a_tiles),
        grid=(a_tiles + b_tiles,),
        in_specs=[pl.BlockSpec((tm, d), lambda i: (jnp.minimum(i, a_tiles - 1), 0)),
                  pl.BlockSpec((tm, d), lambda i: (jnp.maximum(i - a_tiles, 0), 0)),
                  pl.BlockSpec((1, d), lambda i: (0, 0))],
        out_specs=[pl.BlockSpec((tm, d), lambda i: (i, 0)),
                   pl.BlockSpec((tm, d), lambda i: (i, 0))],
        out_shape=[jax.ShapeDtypeStruct((t, d), BF16), jax.ShapeDtypeStruct((t, d), F32)],
        compiler_params=_params("arbitrary"),
    )(xa, xb, g.reshape(1, d))


W_ZERO, W_COPY, W_SHIFT, W_LOW = 0, 1, 2, 3


def _w_in_plan():
    half = LANES // 2
    b0 = A_Q_RANK + A_KV_RANK + A_ROPE
    bw = B_HEADS * 2 * B_QK_DIM
    c0 = b0 + 2 * bw + B_HEADS * B_V_DIM
    cw = C_GROUPS * C_HEADS * HEAD_DIM
    d0 = c0 + 3 * cw
    dq, dk = D_HEADS * HEAD_DIM, D_KV_HEADS * HEAD_DIM
    plan = []

    def shifted(src, width):
        assert src % LANES == half and width % LANES == 0
        for t in range(width // LANES):
            a = (src - half) // LANES + t
            plan.append((W_SHIFT, a, a + 1))

    def pad_to(n):
        while len(plan) * LANES < n:
            plan.append((W_ZERO, 0, 0))

    assert b0 % LANES == half
    for t in range(b0 // LANES):
        plan.append((W_COPY, t, t))
    plan.append((W_LOW, b0 // LANES, b0 // LANES))
    pad_to(N_BV)
    shifted(b0 + 2 * bw, B_HEADS * B_V_DIM)
    shifted(c0 + 2 * cw, cw)
    shifted(d0 + dq + dk, dk)
    pad_to(W_N_COLS)
    shifted(b0, 2 * bw)
    pad_to(W_N_COLS + R64_COLS)
    shifted(c0, 2 * cw)
    shifted(d0, dq + dk)
    pad_to(W_ALL_COLS)
    return tuple(zip(*plan))


def _w_in_prep_kernel(mode_ref, sa_ref, sb_ref, a_ref, b_ref, o_ref):
    mode = mode_ref[pl.program_id(0)]
    low = lax.broadcasted_iota(jnp.int32, (1, LANES), 1) < LANES // 2

    @pl.when(mode == W_ZERO)
    def _():
        o_ref[...] = jnp.zeros_like(o_ref)

    @pl.when(mode == W_COPY)
    def _():
        o_ref[...] = a_ref[...].astype(o_ref.dtype)

    @pl.when(mode == W_LOW)
    def _():
        o_ref[...] = jnp.where(low, a_ref[...], 0.0).astype(o_ref.dtype)

    @pl.when(mode == W_SHIFT)
    def _():
        o_ref[...] = jnp.where(low, pltpu.roll(a_ref[...], LANES // 2, 1),
                               pltpu.roll(b_ref[...], LANES // 2, 1)).astype(o_ref.dtype)


def prepare_w_in(w, layer):
    d = w.shape[1]
    mode, sa, sb = (jnp.asarray(v, jnp.int32) for v in _w_in_plan())
    grid_spec = pltpu.PrefetchScalarGridSpec(
        num_scalar_prefetch=3,
        grid=(W_ALL_COLS // LANES,),
        in_specs=[pl.BlockSpec((None, d, LANES), lambda j, m, a, b: (layer, 0, a[j])),
                  pl.BlockSpec((None, d, LANES), lambda j, m, a, b: (layer, 0, b[j]))],
        out_specs=pl.BlockSpec((d, LANES), lambda j, m, a, b: (0, j)),
    )
    return pl.pallas_call(
        _w_in_prep_kernel,
        grid_spec=grid_spec,
        out_shape=jax.ShapeDtypeStruct((d, W_ALL_COLS), BF16),
        compiler_params=_params("arbitrary"),
    )(mode, sa, sb, w, w)


def _cast_kernel(*refs):
    n = len(refs) // 2
    for x_ref, o_ref in zip(refs[:n], refs[n:]):
        o_ref[...] = x_ref[...].astype(o_ref.dtype)


def cast_experts_bf16(ws, layer, parts=2):
    e = ws[0].shape[1]
    in_specs, out_specs, out_shape = [], [], []
    for w in ws:
        _, _, k, n = w.shape
        in_specs.append(pl.BlockSpec((None, None, k // parts, n), lambda i, r: (layer, i, r, 0)))
        out_specs.append(pl.BlockSpec((None, k // parts, n), lambda i, r: (i, r, 0)))
        out_shape.append(jax.ShapeDtypeStruct((e, k, n), BF16))
    return pl.pallas_call(
        _cast_kernel,
        grid=(e, parts),
        in_specs=in_specs,
        out_specs=out_specs,
        out_shape=out_shape,
        compiler_params=_params("parallel", "parallel"),
    )(*ws)


def _proj_kernel(*refs, half):
    if half is None:
        h_ref, w_ref, o_ref = refs
        o_ref[...] = jnp.dot(h_ref[...], w_ref[...],
                             preferred_element_type=F32).astype(o_ref.dtype)
        return
    h_ref, w_ref, cs_ref, cos_ref, sup_ref, sdn_ref, o_ref = refs
    acc = jnp.dot(h_ref[...], w_ref[...], preferred_element_type=F32) * cs_ref[...]
    cos, s_up, s_dn = cos_ref[...], sup_ref[...], sdn_ref[...]
    for c in range(acc.shape[1] // LANES):
        sl = slice(c * LANES, (c + 1) * LANES)
        o_ref[:, sl] = _rope_tile(acc[:, sl], cos, s_up, s_dn, half).astype(o_ref.dtype)


def in_projection(h, w_all, col0, n, seq, rope=None):
    t, d = h.shape
    tm = _tile(seq, 1024)
    tn = PROJ_TN
    j0 = col0 // tn
    pos_blocks = seq // tm
    in_specs = [pl.BlockSpec((tm, d), lambda i, j: (i, 0)),
                pl.BlockSpec((d, tn), lambda i, j: (0, j0 + j))]
    args = [h, w_all]
    half = None
    if rope is not None:
        half, colscale, cos, s_up, s_dn = rope
        in_specs.append(pl.BlockSpec((1, tn), lambda i, j: (0, j)))
        tab = pl.BlockSpec((tm, LANES), lambda i, j: (i % pos_blocks, 0))
        in_specs += [tab, tab, tab]
        args += [colscale, cos, s_up, s_dn]
    return pl.pallas_call(
        functools.partial(_proj_kernel, half=half),
        grid=(t // tm, n // tn),
        in_specs=in_specs,
        out_specs=pl.BlockSpec((tm, tn), lambda i, j: (i, j)),
        out_shape=jax.ShapeDtypeStruct((t, n), BF16),
        compiler_params=_params("parallel", "parallel"),
    )(*args)


def _mla_proj_kernel(cq_ref, ckv_ref, kpe_ref, qn_ref, kvn_ref, wuq_ref, wukv_ref,
                     cos_ref, sup_ref, sdn_ref, q_ref, k_ref, v_ref, *, scale):
    cos, s_up, s_dn = cos_ref[...], sup_ref[...], sdn_ref[...]

    def normed(ref, g_ref):
        x = ref[...].astype(F32)
        ms = jnp.mean(x * x, axis=-1, keepdims=True)
        return (x * lax.rsqrt(ms + NORM_EPS) * g_ref[...]).astype(BF16)

    q = jnp.dot(normed(cq_ref, qn_ref), wuq_ref[...], preferred_element_type=F32) * scale
    kv = jnp.dot(normed(ckv_ref, kvn_ref), wukv_ref[...], preferred_element_type=F32)
    kpe = _rope_tile(kpe_ref[...].astype(F32), cos, s_up, s_dn, A_ROPE // 2).astype(BF16)
    for h in range(A_HEADS):
        c0 = h * A_QK_PAD
        q_ref[:, c0:c0 + A_NOPE] = q[:, c0:c0 + A_NOPE].astype(BF16)
        q_ref[:, c0 + A_NOPE:c0 + A_QK_PAD] = _rope_tile(
            q[:, c0 + A_NOPE:c0 + A_QK_PAD], cos, s_up, s_dn, A_ROPE // 2).astype(BF16)
        k_ref[:, c0:c0 + A_NOPE] = kv[:, h * A_NOPE:(h + 1) * A_NOPE].astype(BF16)
        k_ref[:, c0 + A_NOPE:c0 + A_QK_PAD] = kpe
    v_ref[...] = kv[:, A_HEADS * A_NOPE:].astype(BF16)


def mla_projection(cols_n, q_norm, kv_norm, w_uq, w_ukv, tabs64, seq):
    t = cols_n.shape[0]
    tm = _tile(seq, 512)
    pos_blocks = seq // tm
    qw = A_HEADS * A_QK_PAD
    vw = A_HEADS * A_V
    tab = pl.BlockSpec((tm, LANES), lambda i: (i % pos_blocks, 0))
    full = lambda shape: pl.BlockSpec(shape, lambda i: (0, 0))
    return pl.pallas_call(
        functools.partial(_mla_proj_kernel, scale=(A_NOPE + A_ROPE) ** -0.5 * LOG2E),
        grid=(t // tm,),
        in_specs=[pl.BlockSpec((tm, A_Q_RANK), lambda i: (i, N_CQ // A_Q_RANK)),
                  pl.BlockSpec((tm, A_KV_RANK), lambda i: (i, N_CKV // A_KV_RANK)),
                  pl.BlockSpec((tm, LANES), lambda i: (i, N_KPE // LANES)),
                  full((1, A_Q_RANK)), full((1, A_KV_RANK)),
                  full((A_Q_RANK, qw)), full((A_KV_RANK, qw)),
                  tab, tab, tab],
        out_specs=[pl.BlockSpec((tm, qw), lambda i: (i, 0)),
                   pl.BlockSpec((tm, qw), lambda i: (i, 0)),
                   pl.BlockSpec((tm, vw), lambda i: (i, 0))],
        out_shape=[jax.ShapeDtypeStruct((t, qw), BF16),
                   jax.ShapeDtypeStruct((t, qw), BF16),
                   jax.ShapeDtypeStruct((t, vw), BF16)],
        compiler_params=_params("parallel"),
    )(cols_n, cols_n, cols_n, q_norm.reshape(1, -1), kv_norm.reshape(1, -1),
      w_uq, w_ukv, *tabs64)


def _softmax_pv(s, v):
    m = jnp.max(s, axis=-1, keepdims=True)
    e = jnp.exp2(s - m)
    l = jnp.sum(e, axis=-1, keepdims=True)
    return jnp.dot(e.astype(BF16), v, preferred_element_type=F32) / l


def _full_attn_kernel(q_ref, k_ref, v_ref, o_ref):
    heads = o_ref.shape[1] // A_V
    s = [_dot_nt(q_ref[:, h * A_QK_PAD:(h + 1) * A_QK_PAD], k_ref[:, h * A_QK_PAD:(h + 1) * A_QK_PAD])
         for h in range(heads)]
    for h in range(heads):
        sl = slice(h * A_V, (h + 1) * A_V)
        o_ref[:, sl] = _softmax_pv(s[h], v_ref[:, sl]).astype(o_ref.dtype)


def mla_attention(q, k, v, nseq, seq):
    tq = _tile(seq, 256)
    nq = seq // tq
    hp = HEADS_PER_STEP
    return pl.pallas_call(
        _full_attn_kernel,
        grid=(nseq, A_HEADS // hp, nq),
        in_specs=[pl.BlockSpec((tq, hp * A_QK_PAD), lambda b, h, i: (b * nq + i, h)),
                  pl.BlockSpec((seq, hp * A_QK_PAD), lambda b, h, i: (b, h)),
                  pl.BlockSpec((seq, hp * A_V), lambda b, h, i: (b, h))],
        out_specs=pl.BlockSpec((tq, hp * A_V), lambda b, h, i: (b * nq + i, h)),
        out_shape=jax.ShapeDtypeStruct((nseq * seq, A_HEADS * A_V), BF16),
        compiler_params=_params("parallel", "parallel", "parallel"),
    )(q, k, v)


def _diff_attn_kernel(q_ref, k_ref, v_ref, lv_ref, g_ref, o_ref, *, lambda_init):
    lv = lv_ref[...]
    lam = (jnp.exp(jnp.sum(lv[0:1] * lv[1:2], axis=-1, keepdims=True))
           - jnp.exp(jnp.sum(lv[2:3] * lv[3:4], axis=-1, keepdims=True)) + lambda_init)
    lane = lax.broadcasted_iota(jnp.int32, (1, LANES), 1)
    heads = o_ref.shape[1] // B_V_DIM
    s = []
    for h in range(heads):
        sl = slice(h * LANES, (h + 1) * LANES)
        q, k = q_ref[:, sl], k_ref[:, sl]
        zero = jnp.zeros_like(q)
        s.append(_dot_nt(jnp.where(lane < B_QK_DIM, q, zero), k))
        s.append(_dot_nt(jnp.where(lane >= B_QK_DIM, q, zero), k))
    for h in range(heads):
        sl = slice(h * B_V_DIM, (h + 1) * B_V_DIM)
        v = v_ref[:, sl]
        o = _softmax_pv(s[2 * h], v) - lam * _softmax_pv(s[2 * h + 1], v)
        ms = jnp.mean(o * o, axis=-1, keepdims=True)
        o = o * lax.rsqrt(ms + B_SUBLN_EPS) * g_ref[...] * (1.0 - lambda_init)
        o_ref[:, sl] = o.astype(o_ref.dtype)


def diff_attention(cols_r64, cols_n, lvec, subln_g, lambda_init, nseq, seq):
    tq = _tile(seq, 256)
    nq = seq // tq
    hp = HEADS_PER_STEP
    w = hp * LANES
    qb, kb, vb = R64_BQ // w, R64_BK // w, N_BV // w
    return pl.pallas_call(
        functools.partial(_diff_attn_kernel, lambda_init=lambda_init),
        grid=(nseq, B_HEADS // hp, nq),
        in_specs=[pl.BlockSpec((tq, w), lambda b, h, i: (b * nq + i, qb + h)),
                  pl.BlockSpec((seq, w), lambda b, h, i: (b, kb + h)),
                  pl.BlockSpec((seq, w), lambda b, h, i: (b, vb + h)),
                  pl.BlockSpec((8, LANES), lambda b, h, i: (0, 0)),
                  pl.BlockSpec((1, B_V_DIM), lambda b, h, i: (0, 0))],
        out_specs=pl.BlockSpec((tq, w), lambda b, h, i: (b * nq + i, h)),
        out_shape=jax.ShapeDtypeStruct((nseq * seq, B_HEADS * B_V_DIM), BF16),
        compiler_params=_params("parallel", "parallel", "parallel"),
    )(cols_r64, cols_r64, cols_n, lvec, subln_g.reshape(1, -1))


def _window_attn_kernel(*refs, length, half_window, has_sink, want_stat, hp, kv_group):
    refs = list(refs)
    sink_ref = refs.pop(0) if has_sink else None
    q_ref, k_ref, v_ref, o_ref = refs[:4]
    stat_ref = refs[4] if want_stat else None
    h0 = pl.program_id(1) * hp
    qb = min(LANES, length)
    kw = min(qb + 2 * half_window, length)
    lane = lax.broadcasted_iota(jnp.int32, (1, LANES), 1)
    ones = _ones_column(kw)

    if want_stat:
        @pl.when(h0 == 0)
        def _():
            stat_ref[...] = jnp.zeros_like(stat_ref)

    def body(i, carry):
        r0 = pl.multiple_of(i * qb, qb)
        ks = pl.multiple_of(jnp.clip(r0 - half_window, 0, length - kw), 64)
        rows, keys = pl.ds(r0, qb), pl.ds(ks, kw)
        qpos = r0 + lax.broadcasted_iota(jnp.int32, (qb, 1), 0)
        kpos = ks + lax.broadcasted_iota(jnp.int32, (1, kw), 1)
        valid = jnp.abs(kpos - qpos) <= half_window
        stat = stat_ref[rows, :] if want_stat else None
        for hh in range(hp):
            qc = slice(hh * HEAD_DIM, (hh + 1) * HEAD_DIM)
            kc = slice(hh // kv_group * HEAD_DIM, (hh // kv_group + 1) * HEAD_DIM)
            s = jnp.where(valid, _dot_nt(q_ref[rows, qc], k_ref[keys, kc]), NEG_INF)
            m = jnp.max(s, axis=-1, keepdims=True)
            if has_sink:
                m = jnp.maximum(m, sink_ref[h0 + hh])
            v = jnp.concatenate([v_ref[keys, kc], ones], axis=-1)
            o = jnp.dot(jnp.exp2(s - m).astype(BF16), v, preferred_element_type=F32)
            l = o[:, HEAD_DIM:HEAD_DIM + 1]
            if has_sink:
                l = l + jnp.exp2(sink_ref[h0 + hh] - m)
            o_ref[rows, qc] = (o[:, :HEAD_DIM] / l).astype(o_ref.dtype)
            if want_stat:
                stat = jnp.where(lane == h0 + hh, m + jnp.log2(l), stat)
        if want_stat:
            stat_ref[rows, :] = stat
        return carry

    n_blocks = length // qb
    lax.fori_loop(0, n_blocks, body, 0, unroll=min(2, n_blocks))


def window_attention(q_arr, q_blk, k_arr, k_blk, v_arr, v_blk, *, nseq, length,
                     heads, kv_group, half_window, sink=None, want_stat=False):
    hp = kv_group
    while hp < heads and length * 2 * hp <= WINDOW_BLOCK_ELEMS:
        hp *= 2
    kvp = hp // kv_group
    assert heads % hp == 0 and q_blk % hp == 0 and k_blk % kvp == 0 and v_blk % kvp == 0
    in_specs = [pl.BlockSpec((length, hp * HEAD_DIM), lambda b, h: (b, q_blk // hp + h)),
                pl.BlockSpec((length, kvp * HEAD_DIM), lambda b, h: (b, k_blk // kvp + h)),
                pl.BlockSpec((length, kvp * HEAD_DIM), lambda b, h: (b, v_blk // kvp + h))]
    args = [q_arr, k_arr, v_arr]
    if sink is not None:
        in_specs.insert(0, pl.BlockSpec(memory_space=pltpu.SMEM))
        args.insert(0, sink.astype(F32))
    out_specs = [pl.BlockSpec((length, hp * HEAD_DIM), lambda b, h: (b, h))]
    out_shape = [jax.ShapeDtypeStruct((nseq * length, heads * HEAD_DIM), BF16)]
    if want_stat:
        out_specs.append(pl.BlockSpec((length, LANES), lambda b, h: (b, 0)))
        out_shape.append(jax.ShapeDtypeStruct((nseq * length, LANES), F32))
    return pl.pallas_call(
        functools.partial(_window_attn_kernel, length=length, half_window=half_window,
                          has_sink=sink is not None, want_stat=want_stat, hp=hp, kv_group=kv_group),
        grid=(nseq, heads // hp),
        in_specs=in_specs,
        out_specs=out_specs,
        out_shape=out_shape,
        compiler_params=_params("parallel", "arbitrary"),
    )(*args)


def _merge_kernel(*refs):
    o_refs, s_refs, out_ref = refs[:C_GROUPS], refs[C_GROUPS:2 * C_GROUPS], refs[-1]
    lse = [r[...] for r in s_refs]
    m = functools.reduce(jnp.maximum, lse)
    w = [jnp.exp2(x - m) for x in lse]
    inv = 1.0 / functools.reduce(lambda a, b: a + b, w)
    w = [x * inv for x in w]
    for h in range(C_HEADS):
        sl = slice(h * HEAD_DIM, (h + 1) * HEAD_DIM)
        acc = w[0][:, h:h + 1] * o_refs[0][:, sl].astype(F32)
        for g in range(1, C_GROUPS):
            acc = acc + w[g][:, h:h + 1] * o_refs[g][:, sl].astype(F32)
        out_ref[:, sl] = acc.astype(out_ref.dtype)


def merge_groups(outs, stats):
    t, w = outs[0].shape
    tm = _tile(t, 512)
    return pl.pallas_call(
        _merge_kernel,
        grid=(t // tm,),
        in_specs=([pl.BlockSpec((tm, w), lambda i: (i, 0))] * C_GROUPS
                  + [pl.BlockSpec((tm, LANES), lambda i: (i, 0))] * C_GROUPS),
        out_specs=pl.BlockSpec((tm, w), lambda i: (i, 0)),
        out_shape=jax.ShapeDtypeStruct((t, w), BF16),
        compiler_params=_params("parallel"),
    )(*outs, *stats)


def _out_proj_kernel(a_ref, b_ref, c_ref, d_ref, w_ref, x_ref, o_ref):
    o = jnp.concatenate([a_ref[...], b_ref[...], c_ref[...], d_ref[...]], axis=-1)
    o_ref[...] = x_ref[...] + jnp.dot(o, w_ref[...], preferred_element_type=F32)


def out_projection(parts, w_o, x):
    t, d = x.shape
    tm = _tile(t, 1024)
    tn = _tile(d, 512)
    in_specs = [pl.BlockSpec((tm, p.shape[1]), lambda i, j: (i, 0)) for p in parts]
    in_specs += [pl.BlockSpec((w_o.shape[0], tn), lambda i, j: (0, j)),
                 pl.BlockSpec((tm, tn), lambda i, j: (i, j))]
    return pl.pallas_call(
        _out_proj_kernel,
        grid=(t // tm, d // tn),
        in_specs=in_specs,
        out_specs=pl.BlockSpec((tm, tn), lambda i, j: (i, j)),
        out_shape=jax.ShapeDtypeStruct((t, d), F32),
        compiler_params=_params("parallel", "parallel"),
    )(*parts, w_o, x)


def _router_kernel(x_ref, g_ref, whi_ref, wlo_ref, b_ref, h_ref, r_ref, c_ref):
    x = x_ref[...]
    ms = jnp.mean(x * x, axis=-1, keepdims=True)
    hn = x * lax.rsqrt(ms + NORM_EPS) * g_ref[...]
    h_ref[...] = hn
    hi = hn.astype(BF16)
    lo = (hn - hi.astype(F32)).astype(BF16)
    w_hi = whi_ref[...]
    logits = (jnp.dot(hi, w_hi, preferred_element_type=F32)
              + jnp.dot(lo, w_hi, preferred_element_type=F32)
              + jnp.dot(hi, wlo_ref[...], preferred_element_type=F32)) + b_ref[...]
    lane = lax.broadcasted_iota(jnp.int32, logits.shape, 1)
    ninf = float("-inf")
    big = jnp.int32(LANES)
    gmask = (lane >= N_EXPERTS) & (lane < N_EXPERTS + N_GROUPS)
    gl = jnp.where(gmask, logits, ninf)
    gmax = jnp.max(gl, axis=-1, keepdims=True)
    g_idx = jnp.min(jnp.where(gl == gmax, lane - N_EXPERTS, big), axis=-1, keepdims=True)
    g_w = 1.0 / jnp.sum(jnp.exp(gl - gmax), axis=-1, keepdims=True)
    emask = (lane < N_EXPERTS) & ((lane // EXPERTS_PER_GROUP) == g_idx)
    el = jnp.where(emask, logits, ninf)
    v1 = jnp.max(el, axis=-1, keepdims=True)
    i1 = jnp.min(jnp.where(el == v1, lane, big), axis=-1, keepdims=True)
    el2 = jnp.where(lane == i1, ninf, el)
    v2 = jnp.max(el2, axis=-1, keepdims=True)
    i2 = jnp.min(jnp.where(el2 == v2, lane, big), axis=-1, keepdims=True)
    t = jnp.exp(v2 - v1)
    w1 = g_w / (1.0 + t)
    w2 = w1 * t
    r_ref[...] = jnp.where(lane == 0, i1.astype(F32),
                           jnp.where(lane == 1, i2.astype(F32),
                                     jnp.where(lane == 2, w1, jnp.where(lane == 3, w2, 0.0))))
    hits = jnp.where((lane == i1) | (lane == i2), 1.0, 0.0)
    sub = lax.broadcasted_iota(jnp.int32, c_ref.shape, 0)
    c_ref[...] = jnp.where(sub == 0, jnp.sum(hits, axis=0, keepdims=True), 0.0)


def norm_and_route(x, g, w_hi, w_lo, bias):
    t, d = x.shape
    tm = _tile(t, 256)
    full = lambda shape: pl.BlockSpec(shape, lambda i: (0, 0))
    h, route, cnt = pl.pallas_call(
        _router_kernel,
        grid=(t // tm,),
        in_specs=[pl.BlockSpec((tm, d), lambda i: (i, 0)), full((1, d)),
                  full((d, LANES)), full((d, LANES)), full((1, LANES))],
        out_specs=[pl.BlockSpec((tm, d), lambda i: (i, 0)),
                   pl.BlockSpec((tm, LANES), lambda i: (i, 0)),
                   pl.BlockSpec((8, LANES), lambda i: (i, 0))],
        out_shape=[jax.ShapeDtypeStruct((t, d), F32),
                   jax.ShapeDtypeStruct((t, LANES), F32),
                   jax.ShapeDtypeStruct((t // tm * 8, LANES), F32)],
        compiler_params=_params("parallel"),
    )(x, g.reshape(1, d), w_hi, w_lo, bias)
    counts = jnp.sum(cnt, axis=0)[:N_EXPERTS].astype(jnp.int32)
    return h, route, counts


def _expert_kernel(te_ref, tv_ref, src_ref, gate_ref, h_hbm, wg_ref, wu_ref, wd_ref, o_ref,
                   xbuf, sem, *, tm):
    j = pl.program_id(0)
    last = pl.num_programs(0) - 1
    slot = j % 2

    def row_copy(tile, r, s):
        tok = src_ref[tile * tm + r]
        return pltpu.make_async_copy(h_hbm.at[pl.ds(tok, 1), :], xbuf.at[s, pl.ds(r, 1), :],
                                     sem.at[s])

    def start_tile(tile, s):
        for r in range(tm):
            row_copy(tile, r, s).start()

    @pl.when((j == 0) & (tv_ref[0] != 0))
    def _():
        start_tile(0, 0)

    nxt = jnp.minimum(j + 1, last)

    @pl.when((j < last) & (tv_ref[nxt] != 0))
    def _():
        start_tile(nxt, 1 - slot)

    @pl.when(tv_ref[j] != 0)
    def _():
        for r in range(tm):
            row_copy(j, r, slot).wait()
        x = xbuf[slot].astype(BF16)
        hg = jnp.dot(x, wg_ref[...], preferred_element_type=F32)
        hu = jnp.dot(x, wu_ref[...], preferred_element_type=F32)
        a = hg / (1.0 + jnp.exp(-hg)) * hu * gate_ref[...]
        o_ref[...] = jnp.dot(a.astype(BF16), wd_ref[...],
                             preferred_element_type=F32).astype(o_ref.dtype)

    @pl.when(tv_ref[j] == 0)
    def _():
        o_ref[...] = jnp.zeros_like(o_ref)


def expert_ffn(h, src_token, gate_rows, tile_expert, tile_valid, w_gate, w_up, w_down, tm):
    d = h.shape[1]
    r = src_token.shape[0]
    ff = w_gate.shape[2]
    grid_spec = pltpu.PrefetchScalarGridSpec(
        num_scalar_prefetch=3,
        grid=(r // tm,),
        in_specs=[pl.BlockSpec((tm, 1), lambda j, te, tv, src: (j, 0)),
                  pl.BlockSpec(memory_space=pl.ANY),
                  pl.BlockSpec((None, d, ff), lambda j, te, tv, src: (te[j], 0, 0)),
                  pl.BlockSpec((None, d, ff), lambda j, te, tv, src: (te[j], 0, 0)),
                  pl.BlockSpec((None, ff, d), lambda j, te, tv, src: (te[j], 0, 0))],
        out_specs=pl.BlockSpec((tm, d), lambda j, te, tv, src: (j, 0)),
        scratch_shapes=[pltpu.VMEM((2, tm, d), F32), pltpu.SemaphoreType.DMA((2,))],
    )
    return pl.pallas_call(
        functools.partial(_expert_kernel, tm=tm),
        grid_spec=grid_spec,
        out_shape=jax.ShapeDtypeStruct((r, d), BF16),
        compiler_params=pltpu.CompilerParams(dimension_semantics=("arbitrary",),
                                             vmem_limit_bytes=VMEM_LIMIT,
                                             disable_bounds_checks=True),
    )(tile_expert, tile_valid, src_token, gate_rows, h, w_gate, w_up, w_down)


def _combine_kernel(x_ref, y1_ref, y2_ref, g_ref, o_ref, *, final):
    x = x_ref[...] + y1_ref[...] + y2_ref[...]
    if final:
        ms = jnp.mean(x * x, axis=-1, keepdims=True)
        x = x * lax.rsqrt(ms + NORM_EPS) * g_ref[...]
    o_ref[...] = x


def combine(x, y1, y2, g, final, row0=0, nrows=None):
    d = x.shape[1]
    nrows = x.shape[0] if nrows is None else nrows
    tm = _tile(math.gcd(nrows, row0) if row0 else nrows, 256)
    i0 = row0 // tm
    row = pl.BlockSpec((tm, d), lambda i: (i0 + i, 0))
    return pl.pallas_call(
        functools.partial(_combine_kernel, final=final),
        grid=(nrows // tm,),
        in_specs=[row, row, row, pl.BlockSpec((1, d), lambda i: (0, 0))],
        out_specs=pl.BlockSpec((tm, d), lambda i: (i, 0)),
        out_shape=jax.ShapeDtypeStruct((nrows, d), F32),
        compiler_params=_params("parallel"),
    )(x, y1, y2, g.reshape(1, d))


def dispatch_plan(route, counts, tm):
    t = route.shape[0]
    e_flat = jnp.concatenate([route[:, 0], route[:, 1]]).astype(jnp.int32)
    g_flat = jnp.concatenate([route[:, 2], route[:, 3]])
    pair = jnp.arange(2 * t, dtype=jnp.int32)
    _, order = lax.sort((e_flat, pair), num_keys=1, is_stable=True)
    start = jnp.cumsum(counts) - counts
    padded = (counts + tm - 1) // tm * tm
    pad_end = jnp.cumsum(padded)
    pad_start = pad_end - padded
    n_tiles = 2 * t // tm + N_EXPERTS
    rows = n_tiles * tm
    tile_start = jnp.arange(n_tiles, dtype=jnp.int32) * tm
    tile_valid = tile_start < pad_end[-1]
    tile_expert = jnp.minimum(
        jnp.sum((pad_end[None, :] <= tile_start[:, None]).astype(jnp.int32), axis=1), N_EXPERTS - 1)
    within = jnp.arange(tm, dtype=jnp.int32)[None, :]
    k = tile_start[:, None] + within - pad_start[tile_expert][:, None]
    valid = ((k < counts[tile_expert][:, None]) & tile_valid[:, None]).reshape(rows)
    sorted_idx = jnp.clip(start[tile_expert][:, None] + k, 0, 2 * t - 1).reshape(rows)
    src_pair = order[sorted_idx]
    src_token = jnp.where(valid, src_pair % t, 0)
    gate_rows = jnp.where(valid, g_flat[src_pair], 0.0)
    row = jnp.arange(rows, dtype=jnp.int32)
    _, pos = lax.sort((jnp.where(valid, src_pair, 2 * t + row), row), num_keys=1)
    return (src_token, gate_rows.reshape(rows, 1), tile_expert,
            tile_valid.astype(jnp.int32), pos[:t], pos[t:2 * t])


def rope_tables(seq, dim):
    inv = 1.0 / (ROPE_THETA ** (jnp.arange(0, dim, 2, dtype=F32) / dim))
    ang = jnp.arange(seq, dtype=F32)[:, None] * inv[None, :]
    cos, sin = jnp.cos(ang), jnp.sin(ang)
    reps = LANES // dim
    zero = jnp.zeros_like(sin)
    cos_t = jnp.tile(jnp.concatenate([cos, cos], axis=1), (1, reps))
    if reps == 1:
        return cos_t, jnp.concatenate([-sin, sin], axis=1), zero
    s_up = jnp.tile(jnp.concatenate([-sin, zero], axis=1), (1, reps))
    s_dn = jnp.tile(jnp.concatenate([zero, sin], axis=1), (1, reps))
    return cos_t, s_up, s_dn


def _pad_cols(w, n):
    return jnp.pad(w, ((0, 0), (0, n - w.shape[1])))


def split_mla_weights(w_uq, w_ukv):
    d = A_NOPE + A_ROPE
    wq = w_uq.reshape(A_Q_RANK, A_HEADS, d)
    wq = jnp.pad(wq, ((0, 0), (0, 0), (0, A_QK_PAD - d))).reshape(A_Q_RANK, A_HEADS * A_QK_PAD)
    wkv = w_ukv.reshape(A_KV_RANK, A_HEADS, A_NOPE + A_V)
    wkv = jnp.concatenate([wkv[:, :, :A_NOPE].reshape(A_KV_RANK, -1),
                           wkv[:, :, A_NOPE:].reshape(A_KV_RANK, -1)], axis=1)
    return wq.astype(BF16), wkv.astype(BF16)


def _residue_major(a, nseq, seq, dil):
    if dil == 1:
        return a
    w = a.shape[1]
    return a.reshape(nseq, seq // dil, dil, w).transpose(0, 2, 1, 3).reshape(nseq * seq, w)


def _natural_order(a, nseq, seq, dil):
    if dil == 1:
        return a
    w = a.shape[1]
    return a.reshape(nseq, dil, seq // dil, w).transpose(0, 2, 1, 3).reshape(nseq * seq, w)


def kernel(x_prompt, x_sample, attn_norm, w_in, mla_q_norm, mla_w_uq, mla_kv_norm, mla_w_ukv, diff_lq1, diff_lk1, diff_lq2, diff_lk2, diff_subln, win_sink, w_o, ffn_norm, router_group_w, router_group_b, router_expert_w, router_expert_b, expert_w_gate, expert_w_up, expert_w_down, final_norm):
    bp, seq, d = x_prompt.shape
    bs = x_sample.shape[0]
    assert x_sample.shape[1:] == (seq, d)
    nseq = bp + bs
    depth = w_in.shape[0]
    tabs64 = rope_tables(seq, A_ROPE)
    tabs128 = rope_tables(seq, HEAD_DIM)
    hscale = HEAD_DIM ** -0.5 * LOG2E
    cs64 = jnp.concatenate([jnp.full((R64_BK,), B_QK_DIM ** -0.5 * LOG2E, F32),
                            jnp.ones((R64_BK,), F32)]).reshape(1, -1)
    cs128 = jnp.concatenate([jnp.full((R128_CK,), hscale, F32), jnp.ones((R128_CK,), F32),
                             jnp.full((R128_DK - R128_DQ,), hscale, F32),
                             jnp.ones((W_R128_COLS - R128_DK,), F32)]).reshape(1, -1)
    cw = C_HEADS * HEAD_DIM
    tm_e = 256

    x = None
    for i in range(depth):
        lambda_init = 0.8 - 0.6 * math.exp(-0.3 * i)
        w_all = prepare_w_in(w_in, i)
        wq, wkv = split_mla_weights(mla_w_uq[i], mla_w_ukv[i])

        if i == 0:
            h, x = stack_and_norm(x_prompt.reshape(bp * seq, d), x_sample.reshape(bs * seq, d),
                                  attn_norm[i])
        else:
            h = rmsnorm_bf16(x, attn_norm[i])
        cols_n = in_projection(h, w_all, 0, W_N_COLS, seq)
        cols_r64 = in_projection(h, w_all, W_N_COLS, R64_COLS, seq,
                                 rope=(B_QK_DIM // 2, cs64, *tabs64))
        cols_r128 = in_projection(h, w_all, W_N_COLS + R64_COLS, W_R128_COLS, seq,
                                  rope=(HEAD_DIM // 2, cs128, *tabs128))

        qa, ka, va = mla_projection(cols_n, mla_q_norm[i], mla_kv_norm[i], wq, wkv, tabs64, seq)
        o_a = mla_attention(qa, ka, va, nseq, seq)

        lvec = jnp.zeros((8, LANES), F32).at[:4, :B_QK_DIM].set(
            jnp.stack([diff_lq1[i], diff_lk1[i], diff_lq2[i], diff_lk2[i]]))
        o_b = diff_attention(cols_r64, cols_n, lvec, diff_subln[i], lambda_init, nseq, seq)

        outs, stats = [], []
        for g, (window, dil) in enumerate(C_PATTERNS):
            qg = _residue_major(cols_r128[:, R128_CQ + g * cw:R128_CQ + (g + 1) * cw], nseq, seq, dil)
            kg = _residue_major(cols_r128[:, R128_CK + g * cw:R128_CK + (g + 1) * cw], nseq, seq, dil)
            vg = _residue_major(cols_n[:, N_CV + g * cw:N_CV + (g + 1) * cw], nseq, seq, dil)
            og, sg = window_attention(qg, 0, kg, 0, vg, 0, nseq=nseq * dil, length=seq // dil,
                                      heads=C_HEADS, kv_group=1,
                                      half_window=window // (2 * dil), want_stat=True)
            outs.append(_natural_order(og, nseq, seq, dil))
            stats.append(_natural_order(sg, nseq, seq, dil))
        o_c = merge_groups(outs, stats)

        (o_d,) = window_attention(cols_r128, R128_DQ // LANES, cols_r128, R128_DK // LANES,
                                  cols_n, N_DV // LANES, nseq=nseq, length=seq,
                                  heads=D_HEADS, kv_group=D_HEADS // D_KV_HEADS,
                                  half_window=D_WINDOW, sink=win_sink[i] * LOG2E)

        x = out_projection([o_a, o_b, o_c, o_d], w_o[i].astype(BF16), x)

        w_r = jnp.concatenate([router_expert_w[i].reshape(d, N_EXPERTS), router_group_w[i]], axis=1)
        w_r = _pad_cols(w_r, LANES)
        w_r_hi = w_r.astype(BF16)
        w_r_lo = (w_r - w_r_hi.astype(F32)).astype(BF16)
        b_r = jnp.concatenate([router_expert_b[i].reshape(-1), router_group_b[i]])
        b_r = jnp.pad(b_r, (0, LANES - b_r.shape[0])).reshape(1, LANES).astype(F32)
        h2, route, counts = norm_and_route(x, ffn_norm[i], w_r_hi, w_r_lo, b_r)

        src_token, gate_rows, tile_expert, tile_valid, pos1, pos2 = dispatch_plan(route, counts, tm_e)
        wg, wu, wd = cast_experts_bf16([expert_w_gate, expert_w_up, expert_w_down], i)
        yg = expert_ffn(h2, src_token, gate_rows, tile_expert, tile_valid, wg, wu, wd, tm_e)
        y1 = yg.at[pos1].get(mode="promise_in_bounds")
        y2 = yg.at[pos2].get(mode="promise_in_bounds")
        if i < depth - 1:
            x = combine(x, y1, y2, final_norm, final=False)
    y_p = combine(x, y1, y2, final_norm, final=True, row0=0, nrows=bp * seq)
    y_s = combine(x, y1, y2, final_norm, final=True, row0=bp * seq, nrows=bs * seq)
    return (y_p.reshape(bp, seq, d), y_s.reshape(bs, seq, d))
```
